```python
import math
import jax, jax.numpy as jnp
from jax import lax
import numpy as np

D_MODEL = 2048
BATCH = 32
SEQ = 256
DEPTH = 2
DEC_BATCH = 8
DEC_SEQ = 4096
PAST_LEN = 256

GRID_W = 64
ATT_WIDTH = D_MODEL // 2
HY_WIDTH = D_MODEL // 4
POOL_WIDTH = D_MODEL - ATT_WIDTH - HY_WIDTH
ATT_HD = 64
ATT_VD = 2 * ATT_HD
N_ATT_HEADS = ATT_WIDTH // ATT_VD
Q_BLOCK = 128
ROPE_THETA = 10000.0
HY_ORDER = 2
HY_SHORT = 3
HY_BANDS = 16
HY_EMB = 1 + 2 * HY_BANDS
HY_FILTER_HIDDEN = 64
HY_DECAY_TARGET = 1e-2
HY_FAST_DECAY = 0.3
HY_SLOW_DECAY = 1.5
HY_DECAY_SHIFT = 0.05
HY_MIN_DECAY = math.log(HY_DECAY_TARGET) / HY_SLOW_DECAY
HY_MAX_DECAY = math.log(HY_DECAY_TARGET) / HY_FAST_DECAY
POOL_WINDOWS = (2, 4, 8, 16)
N_POOL_GROUPS = len(POOL_WINDOWS)
POOL_GW = POOL_WIDTH // N_POOL_GROUPS
FFN_HIDDEN = ((8 * D_MODEL + 3 * 256 - 1) // (3 * 256)) * 256
IN_COLS = 3 * ATT_WIDTH + (HY_ORDER + 1) * HY_WIDTH + POOL_WIDTH
IN_SPLITS = (ATT_WIDTH, 2 * ATT_WIDTH, 3 * ATT_WIDTH, 3 * ATT_WIDTH + (HY_ORDER + 1) * HY_WIDTH)
EPS = 1e-6
F32 = jnp.float32

kernel_name = "hybrid_diffattn_hyena_pool_dit_step"


def rms_norm(x, g):
    xf = x.astype(F32)
    y = xf * lax.rsqrt(jnp.mean(xf * xf, axis=-1, keepdims=True) + EPS)
    return (y * g.astype(F32)).astype(x.dtype)


def axial_rope_tables(n_tokens):
    n_rows = n_tokens // GRID_W
    rows = jnp.repeat(jnp.arange(n_rows), GRID_W).astype(F32)
    cols = jnp.tile(jnp.arange(GRID_W), n_rows).astype(F32)
    quarter = ATT_HD // 4
    inv = ROPE_THETA ** (-jnp.arange(quarter, dtype=F32) / quarter)
    ang = jnp.concatenate([rows[:, None] * inv, cols[:, None] * inv], -1)
    return jnp.cos(ang)[None, :, None, None, :], jnp.sin(ang)[None, :, None, None, :]


def apply_rope(x, cos, sin):
    half = ATT_HD // 2
    x1, x2 = x[..., :half].astype(F32), x[..., half:].astype(F32)
    return jnp.concatenate([x1 * cos - x2 * sin, x2 * cos + x1 * sin], -1).astype(x.dtype)


def diff_attention(q, k, v, lam):
    B, Lq = q.shape[0], q.shape[1]
    nb = Lq // Q_BLOCK
    qb = jnp.moveaxis(q.reshape(B, nb, Q_BLOCK, N_ATT_HEADS, 2, ATT_HD), 1, 0)
    scale = ATT_HD ** -0.5

    def block(qi):
        s = jnp.einsum('bqhjd,bkhjd->bhjqk', qi, k).astype(F32) * scale
        p = jax.nn.softmax(s, axis=-1)
        a = p[:, :, 0] - lam * p[:, :, 1]
        return jnp.einsum('bhqk,bkhe->bqhe', a.astype(v.dtype), v)

    out = lax.map(block, qb)
    return jnp.moveaxis(out, 0, 1).reshape(B, Lq, N_ATT_HEADS, ATT_VD)


def short_conv(u, w, b):
    L = u.shape[1]
    pad = HY_SHORT // 2
    up = jnp.pad(u, ((0, 0), (pad, pad), (0, 0)))
    return sum(up[:, j:j + L] * w[j] for j in range(HY_SHORT)) + b


def hyena_filter_spectrum(L, w1, b1, fr1, w2, b2, fr2, w3):
    t = jnp.linspace(0.0, 1.0, L, dtype=F32)[:, None]
    wpos = 2.0 * math.pi * jnp.arange(L, dtype=F32)[:, None] / L
    bands = jnp.linspace(1e-4, HY_BANDS - 1, HY_BANDS, dtype=F32)[None, :]
    z = jnp.concatenate([t, jnp.cos(bands * wpos), jnp.sin(bands * wpos)], -1)
    hdn = jnp.sin(fr1.astype(F32) * (z @ w1.astype(F32) + b1.astype(F32)))
    hdn = jnp.sin(fr2.astype(F32) * (hdn @ w2.astype(F32) + b2.astype(F32)))
    filt = (hdn @ w3.astype(F32)).reshape(L, 2, HY_ORDER, HY_WIDTH)
    deltas = jnp.abs(jnp.linspace(HY_MIN_DECAY, HY_MAX_DECAY, HY_WIDTH, dtype=F32))
    window = jnp.exp(-t * deltas) + HY_DECAY_SHIFT
    filt = filt * window[:, None, None, :]
    fwd, bwd = filt[:, 0], filt[:, 1]
    k_circ = jnp.concatenate([fwd, jnp.zeros((1, HY_ORDER, HY_WIDTH), F32), bwd[1:][::-1]], 0)
    k_circ = k_circ / jnp.sum(jnp.abs(k_circ), axis=0, keepdims=True)
    return jnp.fft.rfft(k_circ, axis=0)


def hyena_mixer(u, w_sc, b_sc, w1, b1, fr1, w2, b2, fr2, w3, hy_d):
    L = u.shape[1]
    parts = jnp.split(short_conv(u, w_sc, b_sc), HY_ORDER + 1, axis=-1)
    kf = hyena_filter_spectrum(L, w1, b1, fr1, w2, b2, fr2, w3)
    z = parts[0]
    for o in range(HY_ORDER):
        zf = jnp.fft.rfft(z.astype(F32), n=2 * L, axis=1)
        y = jnp.fft.irfft(zf * kf[None, :, o], n=2 * L, axis=1)[:, :L]
        z = parts[o + 1] * (y.astype(z.dtype) + z * hy_d[o])
    return z


def multiscale_pool(u, w_pool, pool_scale):
    B, L, _ = u.shape
    ug = u.reshape(B, L, N_POOL_GROUPS, POOL_GW)
    cs = jnp.concatenate([jnp.zeros((B, 1, N_POOL_GROUPS, POOL_GW), F32),
                          jnp.cumsum(ug.astype(F32), axis=1)], axis=1)
    t = jnp.arange(L)
    means = []
    for g, w in enumerate(POOL_WINDOWS):
        lo = jnp.clip(t - w // 2, 0, L)
        hi = jnp.clip(t - w // 2 + w, 0, L)
        means.append((cs[:, hi, g] - cs[:, lo, g]) / (hi - lo).astype(F32)[None, :, None])
    pooled = jnp.stack(means, axis=2).astype(u.dtype) - ug
    out = jnp.einsum('blgc,gcd->blgd', pooled, w_pool)
    return out.reshape(B, L, POOL_WIDTH) * pool_scale


def trunk_layer(x, cond, rope, ctx_k, ctx_v, lam_init, p):
    B, L, _ = x.shape
    mod = (jax.nn.silu(cond) @ p['w_ada'] + p['b_ada'])[:, None, :]
    sh1, sc1, g1, sh2, sc2, g2 = jnp.split(mod, 6, axis=-1)
    h = rms_norm(x, p['norm1']) * (1 + sc1) + sh1
    q, k, v, u_hy, u_pool = jnp.split(h @ p['w_in'], IN_SPLITS, axis=-1)
    q = rms_norm(q.reshape(B, L, N_ATT_HEADS, 2, ATT_HD), p['q_norm'])
    k = rms_norm(k.reshape(B, L, N_ATT_HEADS, 2, ATT_HD), p['k_norm'])
    v = v.reshape(B, L, N_ATT_HEADS, ATT_VD)
    if rope is not None:
        q = apply_rope(q, *rope)
        k_att = apply_rope(k, *rope)
    else:
        k_att = k
    if ctx_k is None:
        keys, vals = k_att, v
    else:
        keys = jnp.concatenate([k_att, ctx_k.astype(k.dtype)], axis=1)
        vals = jnp.concatenate([v, ctx_v.astype(v.dtype)], axis=1)
    lw = p['w_lambda'].astype(F32)
    lam = jnp.exp(jnp.sum(lw[0] * lw[1])) - jnp.exp(jnp.sum(lw[2] * lw[3])) + lam_init
    att = diff_attention(q, keys, vals, lam)
    att = (rms_norm(att, p['subln']) * (1.0 - lam_init)).reshape(B, L, ATT_WIDTH)
    hy = hyena_mixer(u_hy, p['w_sc'], p['b_sc'], p['hy_w1'], p['hy_b1'], p['hy_fr1'],
                     p['hy_w2'], p['hy_b2'], p['hy_fr2'], p['hy_w3'], p['hy_d'])
    po = multiscale_pool(u_pool, p['w_pool'], p['pool_scale'])
    mix = jnp.concatenate([att, hy, po], axis=-1) @ p['w_out']
    x = x + g1 * mix
    h2 = rms_norm(x, p['norm2']) * (1 + sc2) + sh2
    f = (jax.nn.silu(h2 @ p['w_gate']) * (h2 @ p['w_up'])) @ p['w_down']
    x = x + g2 * f
    return x, k, v


def setup_inputs(seed: int = 0) -> dict:
    key = jax.random.key(seed)
    ks = iter(jax.random.split(key, 48))

    def nrm(shape, scale=1.0):
        return jax.random.normal(next(ks), shape, F32) * scale

    def gain(shape):
        return 1.0 + 0.05 * nrm(shape)

    D = D_MODEL
    return {
        'x_prompt': nrm((BATCH, SEQ, D)),
        'x_sample': nrm((DEC_BATCH, DEC_SEQ, D)),
        'cache_k': nrm((DEC_BATCH, DEPTH, PAST_LEN, N_ATT_HEADS, 2, ATT_HD)),
        'cache_v': nrm((DEC_BATCH, DEPTH, PAST_LEN, N_ATT_HEADS, ATT_VD)),
        'c': nrm((DEC_BATCH, D)),
        'c_ctx': nrm((D,)),
        'w_ada': nrm((DEPTH, D, 6 * D), 0.5 * D ** -0.5),
        'b_ada': nrm((DEPTH, 6 * D), 0.02),
        'norm1': gain((DEPTH, D)),
        'norm2': gain((DEPTH, D)),
        'w_in': nrm((DEPTH, D, IN_COLS), D ** -0.5),
        'q_norm': gain((DEPTH, ATT_HD)),
        'k_norm': gain((DEPTH, ATT_HD)),
        'w_lambda': nrm((DEPTH, 4, ATT_HD), 0.1),
        'subln': gain((DEPTH, ATT_VD)),
        'w_sc': nrm((DEPTH, HY_SHORT, (HY_ORDER + 1) * HY_WIDTH), HY_SHORT ** -0.5),
        'b_sc': nrm((DEPTH, (HY_ORDER + 1) * HY_WIDTH), 0.02),
        'hy_w1': nrm((DEPTH, HY_EMB, HY_FILTER_HIDDEN), HY_EMB ** -0.5),
        'hy_b1': nrm((DEPTH, HY_FILTER_HIDDEN), 0.1),
        'hy_fr1': gain((DEPTH, HY_FILTER_HIDDEN)),
        'hy_w2': nrm((DEPTH, HY_FILTER_HIDDEN, HY_FILTER_HIDDEN), HY_FILTER_HIDDEN ** -0.5),
        'hy_b2': nrm((DEPTH, HY_FILTER_HIDDEN), 0.1),
        'hy_fr2': gain((DEPTH, HY_FILTER_HIDDEN)),
        'hy_w3': nrm((DEPTH, HY_FILTER_HIDDEN, 2 * HY_ORDER * HY_WIDTH), HY_FILTER_HIDDEN ** -0.5),
        'hy_d': nrm((DEPTH, HY_ORDER, HY_WIDTH), 0.5),
        'w_pool': nrm((DEPTH, N_POOL_GROUPS, POOL_GW, POOL_GW), POOL_GW ** -0.5),
        'pool_scale': gain((DEPTH, POOL_WIDTH)),
        'w_out': nrm((DEPTH, D, D), D ** -0.5),
        'w_gate': nrm((DEPTH, D, FFN_HIDDEN), D ** -0.5),
        'w_up': nrm((DEPTH, D, FFN_HIDDEN), D ** -0.5),
        'w_down': nrm((DEPTH, FFN_HIDDEN, D), FFN_HIDDEN ** -0.5),
    }


def reference(x_prompt, x_sample, cache_k, cache_v, c, c_ctx, w_ada, b_ada, norm1, norm2, w_in,
              q_norm, k_norm, w_lambda, subln, w_sc, b_sc, hy_w1, hy_b1, hy_fr1, hy_w2, hy_b2,
              hy_fr2, hy_w3, hy_d, w_pool, pool_scale, w_out, w_gate, w_up, w_down):
    lat_rope = axial_rope_tables(x_sample.shape[1])
    ctx_cond = c_ctx[None, :]
    y_p, y_s = x_prompt, x_sample
    ks, vs = [], []
    for l in range(DEPTH):
        p = dict(w_ada=w_ada[l], b_ada=b_ada[l], norm1=norm1[l], norm2=norm2[l], w_in=w_in[l],
                 q_norm=q_norm[l], k_norm=k_norm[l], w_lambda=w_lambda[l], subln=subln[l],
                 w_sc=w_sc[l], b_sc=b_sc[l], hy_w1=hy_w1[l], hy_b1=hy_b1[l], hy_fr1=hy_fr1[l],
                 hy_w2=hy_w2[l], hy_b2=hy_b2[l], hy_fr2=hy_fr2[l], hy_w3=hy_w3[l], hy_d=hy_d[l],
                 w_pool=w_pool[l], pool_scale=pool_scale[l], w_out=w_out[l], w_gate=w_gate[l],
                 w_up=w_up[l], w_down=w_down[l])
        lam_init = 0.8 - 0.6 * math.exp(-0.3 * l)
        y_p, k_l, v_l = trunk_layer(y_p, ctx_cond, None, None, None, lam_init, p)
        ks.append(k_l)
        vs.append(v_l)
        y_s, _, _ = trunk_layer(y_s, c, lat_rope, cache_k[:, l], cache_v[:, l], lam_init, p)
    state_k = jnp.stack(ks, axis=1)
    state_v = jnp.stack(vs, axis=1)
    return (y_p, y_s, state_k, state_v)
```

```python
import functools
import math

import jax
import jax.numpy as jnp
from jax import lax
from jax.experimental import pallas as pl
from jax.experimental.pallas import tpu as pltpu

F32 = jnp.float32
BF16 = jnp.bfloat16

D_MODEL = 2048
DEPTH = 2
GRID_W = 64
ATT_WIDTH = 1024
HY_WIDTH = 512
POOL_WIDTH = 512
ATT_HD = 64
ATT_VD = 128
N_HEADS = 8
ROPE_THETA = 10000.0
HY_ORDER = 2
HY_BANDS = 16
HY_EMB = 1 + 2 * HY_BANDS
HY_HIDDEN = 64
HY_DECAY_TARGET = 1e-2
HY_FAST_DECAY = 0.3
HY_SLOW_DECAY = 1.5
HY_DECAY_SHIFT = 0.05
HY_MIN_DECAY = math.log(HY_DECAY_TARGET) / HY_SLOW_DECAY
HY_MAX_DECAY = math.log(HY_DECAY_TARGET) / HY_FAST_DECAY
POOL_WINDOWS = (2, 4, 8, 16)
POOL_GW = 128
POOL_HALO = 16
FFN_HIDDEN = 5632
IN_COLS = 5120
EPS = 1e-6
MOD_ROWS = 16
EMB_PAD = 128
LANES = 128
VMEM_LIMIT = 56 * 1024 * 1024


def _cparams(*sem):
    return pltpu.CompilerParams(dimension_semantics=sem, vmem_limit_bytes=VMEM_LIMIT)


def _split(x):
    hi = x.astype(BF16)
    lo = (x - hi.astype(F32)).astype(BF16)
    return hi, lo


def _dot(a, b):
    return jnp.dot(a, b, preferred_element_type=F32)


def _dot3(a, b):
    ah, al = _split(a)
    bh, bl = _split(b)
    return _dot(ah, bh) + _dot(al, bh) + _dot(ah, bl)


def _ada_kernel(c_ref, w_ref, b_ref, o_ref):
    c = c_ref[...]
    a = c * (1.0 / (1.0 + jnp.exp(-c)))
    o_ref[...] = _dot3(a, w_ref[...]) + b_ref[...]


def ada_mod(cond, w_ada, b_ada, tn=1024):
    depth, d, n = w_ada.shape
    rows = cond.shape[0]
    return pl.pallas_call(
        _ada_kernel,
        grid=(depth, n // tn),
        in_specs=[
            pl.BlockSpec((rows, d), lambda l, j: (0, 0)),
            pl.BlockSpec((None, d, tn), lambda l, j: (l, 0, j)),
            pl.BlockSpec((None, 1, tn), lambda l, j: (l, 0, j)),
        ],
        out_specs=pl.BlockSpec((None, rows, tn), lambda l, j: (l, 0, j)),
        out_shape=jax.ShapeDtypeStruct((depth, rows, n), F32),
        compiler_params=_cparams("parallel", "parallel"),
        name="ada_mod",
    )(cond, w_ada, b_ada.reshape(depth, 1, n))


def _mod_spec(part, row_of_tile):
    return pl.BlockSpec((None, 1, D_MODEL), lambda i, *_: (row_of_tile(i) * 6 + part, 0, 0))


def _inproj_kernel(x_ref, g_ref, sc_ref, sh_ref, w_ref, o_ref, h_ref):
    @pl.when(pl.program_id(1) == 0)
    def _():
        x = x_ref[...]
        y = x * lax.rsqrt(jnp.mean(x * x, axis=-1, keepdims=True) + EPS) * g_ref[...]
        h_ref[...] = (y * (1.0 + sc_ref[...]) + sh_ref[...]).astype(BF16)

    o_ref[...] = _dot(h_ref[...], w_ref[...]).astype(o_ref.dtype)


def in_proj(x, norm_g, mod, row_of_tile, w_in, out_dtype, tm=512, tn=512):
    t, d = x.shape
    n = w_in.shape[1]
    return pl.pallas_call(
        _inproj_kernel,
        grid=(t // tm, n // tn),
        in_specs=[
            pl.BlockSpec((tm, d), lambda i, j: (i, 0)),
            pl.BlockSpec((1, d), lambda i, j: (0, 0)),
            _mod_spec(1, row_of_tile),
            _mod_spec(0, row_of_tile),
            pl.BlockSpec((d, tn), lambda i, j: (0, j)),
        ],
        out_specs=pl.BlockSpec((tm, tn), lambda i, j: (i, j)),
        out_shape=jax.ShapeDtypeStruct((t, n), out_dtype),
        scratch_shapes=[pltpu.VMEM((tm, d), BF16)],
        compiler_params=_cparams("parallel", "arbitrary"),
        name="in_proj",
    )(x, norm_g.reshape(1, d), mod, mod, w_in)


def _group_rms(y, lane):
    ss = y * y
    lo = lane < ATT_HD
    s_lo = jnp.sum(jnp.where(lo, ss, 0.0), axis=-1, keepdims=True)
    s_hi = jnp.sum(jnp.where(lo, 0.0, ss), axis=-1, keepdims=True)
    inv = jnp.where(lo, lax.rsqrt(s_lo * (1.0 / ATT_HD) + EPS), lax.rsqrt(s_hi * (1.0 / ATT_HD) + EPS))
    return y * inv


def _rope(y, cos, sin, lane):
    swap = jnp.where((lane & (ATT_HD // 2)) == 0,
                     pltpu.roll(y, LANES - ATT_HD // 2, 1), pltpu.roll(y, ATT_HD // 2, 1))
    return y * cos + swap * sin


def _qkprep_kernel(*refs, rope, emit_state):
    if rope:
        q_ref, k_ref, gq_ref, gk_ref, cos_ref, sin_ref = refs[:6]
        outs = refs[6:]
    else:
        q_ref, k_ref, gq_ref, gk_ref = refs[:4]
        outs = refs[4:]
    qo_ref, ko_ref = outs[0], outs[1]
    tm = q_ref.shape[0]
    lane = lax.broadcasted_iota(jnp.int32, (tm, LANES), 1)
    for h in range(N_HEADS):
        cs = slice(h * LANES, (h + 1) * LANES)
        q = _group_rms(q_ref[:, cs].astype(F32), lane) * gq_ref[...]
        k = _group_rms(k_ref[:, cs].astype(F32), lane) * gk_ref[...]
        if emit_state:
            outs[2][:, cs] = k
        if rope:
            q = _rope(q, cos_ref[...], sin_ref[...], lane)
            k = _rope(k, cos_ref[...], sin_ref[...], lane)
        qo_ref[:, cs] = (q * (ATT_HD ** -0.5)).astype(BF16)
        ko_ref[:, cs] = k.astype(BF16)


def qk_prep(proj, q_gain, k_gain, rope_tabs, seq_len, emit_state, tm=512):
    t = proj.shape[0]
    tm = min(tm, seq_len)
    gq = jnp.tile(q_gain.reshape(1, ATT_HD), (1, 2))
    gk = jnp.tile(k_gain.reshape(1, ATT_HD), (1, 2))
    in_specs = [
        pl.BlockSpec((tm, ATT_WIDTH), lambda i: (i, 0)),
        pl.BlockSpec((tm, ATT_WIDTH), lambda i: (i, 1)),
        pl.BlockSpec((1, LANES), lambda i: (0, 0)),
        pl.BlockSpec((1, LANES), lambda i: (0, 0)),
    ]
    args = [proj, proj, gq, gk]
    if rope_tabs is not None:
        nblk = seq_len // tm
        in_specs += [pl.BlockSpec((tm, LANES), lambda i: (i % nblk, 0))] * 2
        args += list(rope_tabs)
    out_shape = [jax.ShapeDtypeStruct((t, ATT_WIDTH), BF16)] * 2
    out_specs = [pl.BlockSpec((tm, ATT_WIDTH), lambda i: (i, 0))] * 2
    if emit_state:
        out_shape.append(jax.ShapeDtypeStruct((t, ATT_WIDTH), F32))
        out_specs.append(pl.BlockSpec((tm, ATT_WIDTH), lambda i: (i, 0)))
    return pl.pallas_call(
        functools.partial(_qkprep_kernel, rope=rope_tabs is not None, emit_state=emit_state),
        grid=(t // tm,),
        in_specs=in_specs,
        out_specs=out_specs,
        out_shape=out_shape,
        compiler_params=_cparams("parallel"),
        name="qk_prep",
    )(*args)


def rope_tables(n_tokens):
    pos = jnp.arange(n_tokens)
    rows = (pos // GRID_W).astype(F32)
    cols = (pos % GRID_W).astype(F32)
    quarter = ATT_HD // 4
    inv = ROPE_THETA ** (-jnp.arange(quarter, dtype=F32) / quarter)
    ang = jnp.concatenate([rows[:, None] * inv, cols[:, None] * inv], -1)
    cos, sin = jnp.cos(ang), jnp.sin(ang)
    return jnp.tile(cos, (1, 4)), jnp.concatenate([-sin, sin, -sin, sin], -1)


def _attn_kernel(*refs, n_chunks, tk, has_ctx, lam_init):
    if has_ctx:
        q_ref, k_ref, v_ref, ck_ref, cv_ref, lw_ref, sub_ref, o_ref = refs
    else:
        q_ref, k_ref, v_ref, lw_ref, sub_ref, o_ref = refs
    tq = q_ref.shape[0]
    q = q_ref[...]
    lane = lax.broadcasted_iota(jnp.int32, (tq, LANES), 1)
    zero = jnp.zeros_like(q)
    qz = jnp.concatenate([jnp.where(lane < ATT_HD, q, zero), jnp.where(lane < ATT_HD, zero, q)], axis=0)

    def step(kc, vc, carry):
        m, l, acc = carry
        s = lax.dot_general(qz, kc, (((1,), (1,)), ((), ())), preferred_element_type=F32)
        m_new = jnp.maximum(m, jnp.max(s, axis=-1, keepdims=True))
        alpha = jnp.exp(m - m_new)
        p = jnp.exp(s - m_new)
        l = alpha * l + jnp.sum(p, axis=-1, keepdims=True)
        acc = alpha * acc + _dot(p.astype(BF16), vc)
        return m_new, l, acc

    carry = (jnp.full((2 * tq, 1), -jnp.inf, F32), jnp.zeros((2 * tq, 1), F32),
             jnp.zeros((2 * tq, ATT_VD), F32))

    def body(c, carry):
        off = pl.multiple_of(c * tk, tk)
        return step(k_ref[pl.ds(off, tk), :], v_ref[pl.ds(off, tk), :].astype(BF16), carry)

    carry = lax.fori_loop(0, n_chunks, body, carry)
    if has_ctx:
        carry = step(ck_ref[...].astype(BF16), cv_ref[...].astype(BF16), carry)
    _, l, acc = carry

    lw = lw_ref[...]
    lam = (jnp.exp(jnp.sum(lw[0:1] * lw[1:2], axis=-1, keepdims=True))
           - jnp.exp(jnp.sum(lw[2:3] * lw[3:4], axis=-1, keepdims=True)) + lam_init)
    o = acc[:tq] / l[:tq] - lam * (acc[tq:] / l[tq:])
    o = o * lax.rsqrt(jnp.mean(o * o, axis=-1, keepdims=True) + EPS) * sub_ref[...]
    o_ref[...] = (o * (1.0 - lam_init)).astype(o_ref.dtype)


def diff_attn(qn, kn, proj, cache_k, cache_v, layer, w_lambda, subln, lam_init, batch, seq_len,
              tq=256, tk=512):
    t = qn.shape[0]
    tq = min(tq, seq_len)
    tk = min(tk, seq_len)
    nq = seq_len // tq
    has_ctx = cache_k is not None
    v_col0 = 2 * ATT_WIDTH // LANES
    in_specs = [
        pl.BlockSpec((tq, LANES), lambda b, h, i: (b * nq + i, h)),
        pl.BlockSpec((seq_len, LANES), lambda b, h, i: (b, h)),
        pl.BlockSpec((seq_len, LANES), lambda b, h, i: (b, v_col0 + h)),
    ]
    args = [qn, kn, proj]
    if has_ctx:
        past = cache_k.shape[2]
        in_specs += [pl.BlockSpec((None, None, past, LANES), lambda b, h, i: (b, layer, 0, h))] * 2
        args += [cache_k, cache_v]
    in_specs += [pl.BlockSpec((4, ATT_HD), lambda b, h, i: (0, 0)),
                 pl.BlockSpec((1, ATT_VD), lambda b, h, i: (0, 0))]
    args += [w_lambda, subln.reshape(1, ATT_VD)]
    return pl.pallas_call(
        functools.partial(_attn_kernel, n_chunks=seq_len // tk, tk=tk, has_ctx=has_ctx, lam_init=lam_init),
        grid=(batch, N_HEADS, nq),
        in_specs=in_specs,
        out_specs=pl.BlockSpec((tq, LANES), lambda b, h, i: (b * nq + i, h)),
        out_shape=jax.ShapeDtypeStruct((t, ATT_WIDTH), BF16),
        compiler_params=_cparams("parallel", "parallel", "arbitrary"),
        name="diff_attn",
    )(*args)


def _shortconv_kernel(u_ref, prev_ref, next_ref, w_ref, b_ref, v_ref, x1_ref, x2_ref, *, tiles_per_seq):
    i = pl.program_id(0)
    tl = u_ref.shape[0]
    u = u_ref[...].astype(F32)
    first = (i % tiles_per_seq) == 0
    last = (i % tiles_per_seq) == tiles_per_seq - 1
    sub = prev_ref.shape[0]
    prev_row = jnp.where(first, 0.0, prev_ref[sub - 1:sub, :].astype(F32))
    next_row = jnp.where(last, 0.0, next_ref[0:1, :].astype(F32))
    row = lax.broadcasted_iota(jnp.int32, u.shape, 0)
    below = jnp.where(row == 0, prev_row, pltpu.roll(u, 1, 0))
    above = jnp.where(row == tl - 1, next_row, pltpu.roll(u, tl - 1, 0))
    w = w_ref[...]
    y = below * w[0:1] + u * w[1:2] + above * w[2:3] + b_ref[...]
    v_ref[...] = y[:, :HY_WIDTH]
    x1_ref[...] = y[:, HY_WIDTH:2 * HY_WIDTH]
    x2_ref[...] = y[:, 2 * HY_WIDTH:]


def short_conv(proj, w_sc, b_sc, seq_len, tl=256):
    t = proj.shape[0]
    width = 3 * HY_WIDTH
    cb = 3 * ATT_WIDTH // width
    tl = min(tl, seq_len)
    sub = 16
    nsub = tl // sub
    last_sub = t // sub - 1
    part = jax.ShapeDtypeStruct((t, HY_WIDTH), F32)
    return pl.pallas_call(
        functools.partial(_shortconv_kernel, tiles_per_seq=seq_len // tl),
        grid=(t // tl,),
        in_specs=[
            pl.BlockSpec((tl, width), lambda i: (i, cb)),
            pl.BlockSpec((sub, width), lambda i: (jnp.maximum(i * nsub - 1, 0), cb)),
            pl.BlockSpec((sub, width), lambda i: (jnp.minimum((i + 1) * nsub, last_sub), cb)),
            pl.BlockSpec((3, width), lambda i: (0, 0)),
            pl.BlockSpec((1, width), lambda i: (0, 0)),
        ],
        out_specs=[pl.BlockSpec((tl, HY_WIDTH), lambda i: (i, 0))] * 3,
        out_shape=[part] * 3,
        compiler_params=_cparams("parallel"),
        name="short_conv",
    )(proj, proj, proj, w_sc, b_sc.reshape(1, width))


def _filter_kernel(z_ref, w1_ref, b1_ref, f1_ref, w2_ref, b2_ref, f2_ref, w3f_ref, w3b_ref, dl_ref,
                   o_ref, h_ref, *, seq_len):
    @pl.when(pl.program_id(0) == 0)
    def _():
        h = jnp.sin(f1_ref[...] * (_dot3(z_ref[...], w1_ref[...]) + b1_ref[...]))
        h_ref[...] = jnp.sin(f2_ref[...] * (_dot3(h, w2_ref[...]) + b2_ref[...]))

    n = 2 * seq_len
    t = z_ref[:, 0:1]
    window = jnp.exp(-t * dl_ref[...]) + HY_DECAY_SHIFT
    fwd = _dot3(h_ref[0:seq_len, :], w3f_ref[...])
    bwd = _dot3(h_ref[seq_len:n, :], w3b_ref[...])
    row = lax.broadcasted_iota(jnp.int32, (seq_len, 1), 0)
    bwd = jnp.where(row == 0, 0.0, bwd)
    fwd = fwd * window[0:seq_len]
    bwd = bwd * window[seq_len:n]
    norm = jnp.sum(jnp.abs(fwd), axis=0, keepdims=True) + jnp.sum(jnp.abs(bwd), axis=0, keepdims=True)
    o_ref[0:seq_len, :] = fwd / norm
    o_ref[seq_len:n, :] = bwd / norm


def hyena_embedding(seq_len):
    pos = jnp.concatenate([jnp.arange(seq_len), seq_len - jnp.arange(seq_len)]) % seq_len
    t = jnp.linspace(0.0, 1.0, seq_len, dtype=F32)[pos][:, None]
    wpos = (2.0 * math.pi * jnp.arange(seq_len, dtype=F32) / seq_len)[pos][:, None]
    bands = jnp.linspace(1e-4, HY_BANDS - 1, HY_BANDS, dtype=F32)[None, :]
    z = jnp.concatenate([t, jnp.cos(bands * wpos), jnp.sin(bands * wpos)], -1)
    return jnp.pad(z, ((0, 0), (0, EMB_PAD - HY_EMB)))


def hyena_filter(seq_len, w1, b1, fr1, w2, b2, fr2, w3, tn=128):
    n = 2 * seq_len
    z = hyena_embedding(seq_len)
    w1p = jnp.pad(w1, ((0, EMB_PAD - HY_EMB), (0, 0)))
    deltas = jnp.abs(jnp.linspace(HY_MIN_DECAY, HY_MAX_DECAY, HY_WIDTH, dtype=F32)).reshape(1, HY_WIDTH)
    cpo = HY_WIDTH // tn
    nblk = HY_ORDER * cpo
    row = lambda a: a.reshape(1, -1)
    full = lambda shape: pl.BlockSpec(shape, lambda j: (0,) * len(shape))
    return pl.pallas_call(
        functools.partial(_filter_kernel, seq_len=seq_len),
        grid=(nblk,),
        in_specs=[
            full((n, EMB_PAD)), full((EMB_PAD, HY_HIDDEN)), full((1, HY_HIDDEN)), full((1, HY_HIDDEN)),
            full((HY_HIDDEN, HY_HIDDEN)), full((1, HY_HIDDEN)), full((1, HY_HIDDEN)),
            pl.BlockSpec((HY_HIDDEN, tn), lambda j: (0, j)),
            pl.BlockSpec((HY_HIDDEN, tn), lambda j: (0, nblk + j)),
            pl.BlockSpec((1, tn), lambda j: (0, j % cpo)),
        ],
        out_specs=pl.BlockSpec((None, n, tn), lambda j: (j // cpo, 0, j % cpo)),
        out_shape=jax.ShapeDtypeStruct((HY_ORDER, n, HY_WIDTH), F32),
        scratch_shapes=[pltpu.VMEM((n, HY_HIDDEN), F32)],
        compiler_params=_cparams("arbitrary"),
        name="hyena_filter",
    )(z, w1p, row(b1), row(fr1), w2, row(b2), row(fr2), w3, w3, deltas)


def dft_split(seq_len):
    return (64, 128) if seq_len >= 4096 else (16, 2 * seq_len // 16)


def _angles(a, b, period):
    return (2.0 * math.pi / period) * ((a * b) % period).astype(F32)


def dft_tables(seq_len):
    n1, n2 = dft_split(seq_len)
    n = n1 * n2
    k1 = jnp.arange(n1)
    th = _angles(k1[:, None], k1[None, :], n1)
    c, s = jnp.cos(th), jnp.sin(th)
    fa = jnp.stack([c, -s], axis=1).reshape(2 * n1, n1)
    fc = jnp.stack([c, -s], axis=2).reshape(n1, 2 * n1)[: n1 // 2] / n
    kk = k1[:, None, None] + n1 * jnp.arange(n2)[None, :, None]
    th = _angles(kk, jnp.arange(n2)[None, None, :], n)
    c, s = jnp.cos(th), jnp.sin(th)
    gf = jnp.concatenate([jnp.concatenate([c, s], 2), jnp.concatenate([-s, c], 2)], 1)
    ct, st = jnp.swapaxes(c, 1, 2), jnp.swapaxes(s, 1, 2)
    gi = jnp.concatenate([jnp.concatenate([ct, -st], 2), jnp.concatenate([st, ct], 2)], 1)
    return dict(n1=n1, n2=n2, fa=fa, fc=fc.astype(BF16), gf=_split(gf), gi=gi.astype(BF16))


def _stage_a_kernel(x_ref, f_ref, o_ref, *, precise):
    if precise:
        o_ref[...] = _dot3(f_ref[...], x_ref[...])
    else:
        o_ref[...] = _dot(f_ref[...].astype(BF16), x_ref[...].astype(BF16)).astype(o_ref.dtype)


def stage_a(x, f, precise, tc=8192):
    b, k, cols = x.shape
    m = f.shape[0]
    tc = min(tc, cols)
    return pl.pallas_call(
        functools.partial(_stage_a_kernel, precise=precise),
        grid=(b, cols // tc),
        in_specs=[pl.BlockSpec((None, k, tc), lambda i, j: (i, 0, j)),
                  pl.BlockSpec((m, k), lambda i, j: (0, 0))],
        out_specs=pl.BlockSpec((None, m, tc), lambda i, j: (i, 0, j)),
        out_shape=jax.ShapeDtypeStruct((b, m, cols), F32 if precise else BF16),
        compiler_params=_cparams("parallel", "parallel"),
        name="hyena_stage_a",
    )(x, f)


def _spectrum_kernel(a_ref, gh_ref, gl_ref, o_ref):
    for j in range(a_ref.shape[0]):
        ah, al = _split(a_ref[j])
        o_ref[j] = _dot(gh_ref[j], ah) + _dot(gh_ref[j], al) + _dot(gl_ref[j], ah)


def filter_spectrum(a, gh, gl, tk1=8):
    o, n1, m, c = a.shape
    tk1 = min(tk1, n1)
    return pl.pallas_call(
        _spectrum_kernel,
        grid=(n1 // tk1, o),
        in_specs=[pl.BlockSpec((None, tk1, m, c), lambda i, b: (b, i, 0, 0)),
                  pl.BlockSpec((tk1, m, m), lambda i, b: (i, 0, 0)),
                  pl.BlockSpec((tk1, m, m), lambda i, b: (i, 0, 0))],
        out_specs=pl.BlockSpec((None, tk1, m, c), lambda i, b: (b, i, 0, 0)),
        out_shape=jax.ShapeDtypeStruct(a.shape, F32),
        compiler_params=_cparams("parallel", "arbitrary"),
        name="hyena_filter_spectrum",
    )(a, gh, gl)


def _stage_b_kernel(a_ref, gf_ref, gi_ref, kf_ref, o_ref):
    half = a_ref.shape[1] // 2
    for j in range(a_ref.shape[0]):
        x = _dot(gf_ref[j], a_ref[j])
        kf = kf_ref[j]
        xr, xi = x[:half], x[half:]
        kr, ki = kf[:half], kf[half:]
        p = jnp.concatenate([xr * kr - xi * ki, xr * ki + xi * kr], axis=0).astype(BF16)
        o_ref[j] = _dot(gi_ref[j], p).astype(o_ref.dtype)


def stage_b(a, gf, gi, kf, order, tk1=8):
    b, n1, m, c = a.shape
    tk1 = min(tk1, n1)
    return pl.pallas_call(
        _stage_b_kernel,
        grid=(n1 // tk1, b),
        in_specs=[pl.BlockSpec((None, tk1, m, c), lambda i, bb: (bb, i, 0, 0)),
                  pl.BlockSpec((tk1, m, m), lambda i, bb: (i, 0, 0)),
                  pl.BlockSpec((tk1, m, m), lambda i, bb: (i, 0, 0)),
                  pl.BlockSpec((None, tk1, m, c), lambda i, bb: (order, i, 0, 0))],
        out_specs=pl.BlockSpec((None, tk1, m, c), lambda i, bb: (bb, i, 0, 0)),
        out_shape=jax.ShapeDtypeStruct(a.shape, BF16),
        compiler_params=_cparams("parallel", "arbitrary"),
        name="hyena_stage_b",
    )(a, gf, gi, kf)


def _stage_c_kernel(b_ref, f_ref, x_ref, z_ref, d_ref, o_ref):
    y = _dot(f_ref[...], b_ref[...])
    z = z_ref[...]
    o_ref[...] = (x_ref[...] * (y + z * d_ref[...])).astype(o_ref.dtype)


def stage_c(bm, fc, x, z, d_tiled, out_dtype, tc=8192):
    b, m, cols = bm.shape
    k = fc.shape[0]
    tc = min(tc, cols)
    return pl.pallas_call(
        _stage_c_kernel,
        grid=(b, cols // tc),
        in_specs=[pl.BlockSpec((None, m, tc), lambda i, j: (i, 0, j)),
                  pl.BlockSpec((k, m), lambda i, j: (0, 0)),
                  pl.BlockSpec((None, k, tc), lambda i, j: (i, 0, j)),
                  pl.BlockSpec((None, k, tc), lambda i, j: (i, 0, j)),
                  pl.BlockSpec((1, tc), lambda i, j: (0, j))],
        out_specs=pl.BlockSpec((None, k, tc), lambda i, j: (i, 0, j)),
        out_shape=jax.ShapeDtypeStruct((b, k, cols), out_dtype),
        compiler_params=_cparams("parallel", "parallel"),
        name="hyena_stage_c",
    )(bm, fc, x, z, d_tiled)


def hyena_mixer(proj, batch, seq_len, tabs, w_sc, b_sc, w1, b1, fr1, w2, b2, fr2, w3, hy_d):
    n1, n2 = tabs["n1"], tabs["n2"]
    c = HY_WIDTH
    cols = n2 * c
    kc = hyena_filter(seq_len, w1, b1, fr1, w2, b2, fr2, w3)
    ka = stage_a(kc.reshape(HY_ORDER, n1, cols), tabs["fa"], True)
    kf = filter_spectrum(ka.reshape(HY_ORDER, n1, 2 * n2, c), *tabs["gf"])

    v, x1, x2 = short_conv(proj, w_sc, b_sc, seq_len)
    view = lambda a: a.reshape(batch, n1 // 2, cols)
    z = view(v)
    gates = (view(x1), view(x2))
    fa_half = tabs["fa"][:, : n1 // 2]
    for o in range(HY_ORDER):
        a = stage_a(z, fa_half, False)
        bm = stage_b(a.reshape(batch, n1, 2 * n2, c), tabs["gf"][0], tabs["gi"], kf, o)
        d_tiled = jnp.tile(hy_d[o].reshape(1, c), (1, n2))
        z = stage_c(bm.reshape(batch, 2 * n1, cols), tabs["fc"], gates[o], z, d_tiled,
                    BF16 if o == HY_ORDER - 1 else F32)
    return z.reshape(batch * seq_len, c)


def _pool_kernel(u_ref, w_ref, s_ref, o_ref, pad_ref):
    seq_len = u_ref.shape[0]
    h = POOL_HALO
    zeros = jnp.zeros((h, POOL_WIDTH), F32)
    pad_ref[0:h, :] = zeros
    pad_ref[h + seq_len:h + seq_len + h, :] = zeros
    pad_ref[h:h + seq_len, :] = u_ref[...].astype(F32)
    t = lax.broadcasted_iota(jnp.int32, (seq_len, 1), 0)
    for g, w in enumerate(POOL_WINDOWS):
        cs = slice(g * POOL_GW, (g + 1) * POOL_GW)
        tot = pad_ref[h - w // 2:h - w // 2 + seq_len, cs]
        for d in range(1 - w // 2, w - w // 2):
            tot = tot + pad_ref[h + d:h + d + seq_len, cs]
        cnt = (jnp.minimum(t - w // 2 + w, seq_len) - jnp.maximum(t - w // 2, 0)).astype(F32)
        pooled = tot / cnt - pad_ref[h:h + seq_len, cs]
        o_ref[:, cs] = (_dot(pooled.astype(BF16), w_ref[g]) * s_ref[:, cs]).astype(o_ref.dtype)


def pool_mixer(proj, w_pool, pool_scale, batch, seq_len):
    t = proj.shape[0]
    cb = (IN_COLS - POOL_WIDTH) // POOL_WIDTH
    return pl.pallas_call(
        _pool_kernel,
        grid=(batch,),
        in_specs=[pl.BlockSpec((seq_len, POOL_WIDTH), lambda b: (b, cb)),
                  pl.BlockSpec((len(POOL_WINDOWS), POOL_GW, POOL_GW), lambda b: (0, 0, 0)),
                  pl.BlockSpec((1, POOL_WIDTH), lambda b: (0, 0))],
        out_specs=pl.BlockSpec((seq_len, POOL_WIDTH), lambda b: (b, 0)),
        out_shape=jax.ShapeDtypeStruct((t, POOL_WIDTH), BF16),
        scratch_shapes=[pltpu.VMEM((seq_len + 2 * POOL_HALO, POOL_WIDTH), F32)],
        compiler_params=_cparams("parallel"),
        name="pool_mixer",
    )(proj, w_pool.astype(BF16), pool_scale.reshape(1, POOL_WIDTH))


def _outproj_kernel(att_ref, hy_ref, po_ref, x_ref, wa_ref, wh_ref, wp_ref, g1_ref, g_ref, sc_ref, sh_ref,
                    x1_ref, h2_ref):
    mix = _dot(att_ref[...], wa_ref[...]) + _dot(hy_ref[...], wh_ref[...]) + _dot(po_ref[...], wp_ref[...])
    x1 = x_ref[...] + g1_ref[...] * mix
    x1_ref[...] = x1
    y = x1 * lax.rsqrt(jnp.mean(x1 * x1, axis=-1, keepdims=True) + EPS) * g_ref[...]
    h2_ref[...] = (y * (1.0 + sc_ref[...]) + sh_ref[...]).astype(BF16)


def out_proj(att, hy, po, x, w_out, norm_g, mod, row_of_tile, tm=512):
    t, d = x.shape
    a0, a1 = ATT_WIDTH, ATT_WIDTH + HY_WIDTH
    const = lambda shape: pl.BlockSpec(shape, lambda i: (0, 0))
    return pl.pallas_call(
        _outproj_kernel,
        grid=(t // tm,),
        in_specs=[
            pl.BlockSpec((tm, ATT_WIDTH), lambda i: (i, 0)),
            pl.BlockSpec((tm, HY_WIDTH), lambda i: (i, 0)),
            pl.BlockSpec((tm, POOL_WIDTH), lambda i: (i, 0)),
            pl.BlockSpec((tm, d), lambda i: (i, 0)),
            const((ATT_WIDTH, d)), const((HY_WIDTH, d)), const((POOL_WIDTH, d)),
            _mod_spec(2, row_of_tile),
            const((1, d)),
            _mod_spec(4, row_of_tile),
            _mod_spec(3, row_of_tile),
        ],
        out_specs=[pl.BlockSpec((tm, d), lambda i: (i, 0))] * 2,
        out_shape=[jax.ShapeDtypeStruct((t, d), F32), jax.ShapeDtypeStruct((t, d), BF16)],
        compiler_params=_cparams("parallel"),
        name="out_proj",
    )(att, hy, po, x, w_out[:a0], w_out[a0:a1], w_out[a1:], mod, norm_g.reshape(1, d), mod, mod)


def _ffn_kernel(h_ref, x_ref, wg_ref, wu_ref, wd_ref, g2_ref, o_ref, acc_ref):
    j = pl.program_id(1)

    @pl.when(j == 0)
    def _():
        acc_ref[...] = jnp.zeros_like(acc_ref)

    h = h_ref[...]
    gate = _dot(h, wg_ref[...])
    up = _dot(h, wu_ref[...])
    act = (gate * (1.0 / (1.0 + jnp.exp(-gate))) * up).astype(BF16)
    acc_ref[...] += _dot(act, wd_ref[...])

    @pl.when(j == pl.num_programs(1) - 1)
    def _():
        o_ref[...] = x_ref[...] + g2_ref[...] * acc_ref[...]


def ffn(h2, x1, w_gate, w_up, w_down, mod, row_of_tile, tm=512, th=512):
    t, d = x1.shape
    hidden = w_gate.shape[1]
    return pl.pallas_call(
        _ffn_kernel,
        grid=(t // tm, hidden // th),
        in_specs=[
            pl.BlockSpec((tm, d), lambda i, j: (i, 0)),
            pl.BlockSpec((tm, d), lambda i, j: (i, 0)),
            pl.BlockSpec((d, th), lambda i, j: (0, j)),
            pl.BlockSpec((d, th), lambda i, j: (0, j)),
            pl.BlockSpec((th, d), lambda i, j: (j, 0)),
            _mod_spec(5, row_of_tile),
        ],
        out_specs=pl.BlockSpec((tm, d), lambda i, j: (i, 0)),
        out_shape=jax.ShapeDtypeStruct((t, d), F32),
        scratch_shapes=[pltpu.VMEM((tm, d), F32)],
        compiler_params=_cparams("parallel", "arbitrary"),
        name="ffn",
    )(h2, x1, w_gate, w_up, w_down, mod)


def trunk_layer(x, batch, seq_len, mod, row_of_tile, rope_tabs, cache, layer, lam_init, tabs, p,
                emit_state, tm=512):
    tile_row = lambda i: row_of_tile(i * tm)
    proj = in_proj(x, p["norm1"], mod, tile_row, p["w_in"], F32 if emit_state else BF16, tm=tm)
    prep = qk_prep(proj, p["q_norm"], p["k_norm"], rope_tabs, seq_len, emit_state)
    cache_k, cache_v = cache if cache is not None else (None, None)
    att = diff_attn(prep[0], prep[1], proj, cache_k, cache_v, layer, p["w_lambda"], p["subln"], lam_init,
                    batch, seq_len)
    hy = hyena_mixer(proj, batch, seq_len, tabs, p["w_sc"], p["b_sc"], p["hy_w1"], p["hy_b1"], p["hy_fr1"],
                     p["hy_w2"], p["hy_b2"], p["hy_fr2"], p["hy_w3"], p["hy_d"])
    po = pool_mixer(proj, p["w_pool"], p["pool_scale"], batch, seq_len)
    x1, h2 = out_proj(att, hy, po, x, p["w_out"], p["norm2"], mod, tile_row, tm=tm)
    y = ffn(h2, x1, p["w_gate"], p["w_up"], p["w_down"], mod, tile_row, tm=tm)
    if emit_state:
        return y, prep[2], proj[:, 2 * ATT_WIDTH:3 * ATT_WIDTH]
    return y, None, None


def kernel(x_prompt, x_sample, cache_k, cache_v, c, c_ctx, w_ada, b_ada, norm1, norm2, w_in, q_norm, k_norm, w_lambda, subln, w_sc, b_sc, hy_w1, hy_b1, hy_fr1, hy_w2, hy_b2, hy_fr2, hy_w3, hy_d, w_pool, pool_scale, w_out, w_gate, w_up, w_down):
    batch, seq, d = x_prompt.shape
    dec_batch, dec_seq, _ = x_sample.shape
    depth = w_ada.shape[0]
    past = cache_k.shape[2]

    cond = jnp.concatenate([c_ctx[None, :], c, jnp.zeros((MOD_ROWS - 1 - dec_batch, d), F32)], axis=0)
    mod_all = ada_mod(cond, w_ada, b_ada)
    rope_tabs = rope_tables(dec_seq)
    tabs_ctx, tabs_lat = dft_tables(seq), dft_tables(dec_seq)
    ck = cache_k.reshape(dec_batch, depth, past, ATT_WIDTH)
    cv = cache_v.reshape(dec_batch, depth, past, ATT_WIDTH)

    y_p = x_prompt.reshape(batch * seq, d)
    y_s = x_sample.reshape(dec_batch * dec_seq, d)
    ks, vs = [], []
    for l in range(depth):
        p = dict(norm1=norm1[l], norm2=norm2[l], w_in=w_in[l].astype(BF16), q_norm=q_norm[l], k_norm=k_norm[l],
                 w_lambda=w_lambda[l], subln=subln[l], w_sc=w_sc[l], b_sc=b_sc[l], hy_w1=hy_w1[l],
                 hy_b1=hy_b1[l], hy_fr1=hy_fr1[l], hy_w2=hy_w2[l], hy_b2=hy_b2[l], hy_fr2=hy_fr2[l],
                 hy_w3=hy_w3[l], hy_d=hy_d[l], w_pool=w_pool[l], pool_scale=pool_scale[l],
                 w_out=w_out[l].astype(BF16), w_gate=w_gate[l].astype(BF16), w_up=w_up[l].astype(BF16),
                 w_down=w_down[l].astype(BF16))
        lam_init = 0.8 - 0.6 * math.exp(-0.3 * l)
        mod = mod_all[l].reshape(MOD_ROWS * 6, 1, d)
        y_p, k_l, v_l = trunk_layer(y_p, batch, seq, mod, lambda tok: 0, None, None, l, lam_init,
                                    tabs_ctx, p, True)
        ks.append(k_l.reshape(batch, seq, N_HEADS, 2, ATT_HD))
        vs.append(v_l.reshape(batch, seq, N_HEADS, ATT_VD))
        y_s, _, _ = trunk_layer(y_s, dec_batch, dec_seq, mod, lambda tok: 1 + tok // dec_seq, rope_tabs,
                                (ck, cv), l, lam_init, tabs_lat, p, False)
    return (y_p.reshape(batch, seq, d), y_s.reshape(dec_batch, dec_seq, d),
            jnp.stack(ks, axis=1), jnp.stack(vs, axis=1))
```

```python
import functools
import math

import jax
import jax.numpy as jnp
from jax import lax
from jax.experimental import pallas as pl
from jax.experimental.pallas import tpu as pltpu

F32 = jnp.float32
BF16 = jnp.bfloat16

D_MODEL = 2048
DEPTH = 2
GRID_W = 64
ATT_WIDTH = 1024
HY_WIDTH = 512
POOL_WIDTH = 512
ATT_HD = 64
ATT_VD = 128
N_HEADS = 8
ROPE_THETA = 10000.0
HY_ORDER = 2
HY_BANDS = 16
HY_EMB = 1 + 2 * HY_BANDS
HY_HIDDEN = 64
HY_DECAY_TARGET = 1e-2
HY_FAST_DECAY = 0.3
HY_SLOW_DECAY = 1.5
HY_DECAY_SHIFT = 0.05
HY_MIN_DECAY = math.log(HY_DECAY_TARGET) / HY_SLOW_DECAY
HY_MAX_DECAY = math.log(HY_DECAY_TARGET) / HY_FAST_DECAY
POOL_WINDOWS = (2, 4, 8, 16)
POOL_GW = 128
POOL_HALO = 16
FFN_HIDDEN = 5632
IN_COLS = 5120
EPS = 1e-6
Q_SCALE = ATT_HD ** -0.5 * math.log2(math.e)
MOD_ROWS = 16
EMB_PAD = 128
LANES = 128
VMEM_LIMIT = 56 * 1024 * 1024


def _cparams(*sem):
    return pltpu.CompilerParams(dimension_semantics=sem, vmem_limit_bytes=VMEM_LIMIT)


def _split(x):
    hi = x.astype(BF16)
    lo = (x - hi.astype(F32)).astype(BF16)
    return hi, lo


def _dot(a, b):
    return jnp.dot(a, b, preferred_element_type=F32)


def _dot3(a, b):
    ah, al = _split(a)
    bh, bl = _split(b)
    return _dot(ah, bh) + _dot(al, bh) + _dot(ah, bl)


def _ada_kernel(c_ref, w_ref, b_ref, o_ref):
    c = c_ref[...]
    a = c * (1.0 / (1.0 + jnp.exp(-c)))
    o_ref[...] = _dot3(a, w_ref[...]) + b_ref[...]


def ada_mod(cond, w_ada, b_ada, tn=1024):
    depth, d, n = w_ada.shape
    rows = cond.shape[0]
    return pl.pallas_call(
        _ada_kernel,
        grid=(depth, n // tn),
        in_specs=[
            pl.BlockSpec((rows, d), lambda l, j: (0, 0)),
            pl.BlockSpec((None, d, tn), lambda l, j: (l, 0, j)),
            pl.BlockSpec((None, 1, tn), lambda l, j: (l, 0, j)),
        ],
        out_specs=pl.BlockSpec((None, rows, tn), lambda l, j: (l, 0, j)),
        out_shape=jax.ShapeDtypeStruct((depth, rows, n), F32),
        compiler_params=_cparams("parallel", "parallel"),
        name="ada_mod",
    )(cond, w_ada, b_ada.reshape(depth, 1, n))


def _mod_spec(part, row_of_token, tm):
    return pl.BlockSpec((None, 1, D_MODEL), lambda i, *_: (row_of_token(i * tm) * 6 + part, 0, 0))


def _inproj_kernel(x_ref, g_ref, sc_ref, sh_ref, w_ref, o_ref, h_ref):
    @pl.when(pl.program_id(1) == 0)
    def _():
        x = x_ref[...]
        y = x * lax.rsqrt(jnp.mean(x * x, axis=-1, keepdims=True) + EPS) * g_ref[...]
        h_ref[...] = (y * (1.0 + sc_ref[...]) + sh_ref[...]).astype(BF16)

    o_ref[...] = _dot(h_ref[...], w_ref[...]).astype(o_ref.dtype)


def in_proj(x, norm_g, mod, row_of_token, w_in, out_dtype, tm=1024, tn=1024):
    t, d = x.shape
    n = w_in.shape[1]
    tm = min(tm, t)
    return pl.pallas_call(
        _inproj_kernel,
        grid=(t // tm, n // tn),
        in_specs=[
            pl.BlockSpec((tm, d), lambda i, j: (i, 0)),
            pl.BlockSpec((1, d), lambda i, j: (0, 0)),
            _mod_spec(1, row_of_token, tm),
            _mod_spec(0, row_of_token, tm),
            pl.BlockSpec((d, tn), lambda i, j: (0, j)),
        ],
        out_specs=pl.BlockSpec((tm, tn), lambda i, j: (i, j)),
        out_shape=jax.ShapeDtypeStruct((t, n), out_dtype),
        scratch_shapes=[pltpu.VMEM((tm, d), BF16)],
        compiler_params=_cparams("parallel", "arbitrary"),
        name="in_proj",
    )(x, norm_g.reshape(1, d), mod, mod, w_in)


def _group_rms(y, lane):
    ss = y * y
    lo = lane < ATT_HD
    s_lo = jnp.sum(jnp.where(lo, ss, 0.0), axis=-1, keepdims=True)
    s_hi = jnp.sum(jnp.where(lo, 0.0, ss), axis=-1, keepdims=True)
    inv = jnp.where(lo, lax.rsqrt(s_lo * (1.0 / ATT_HD) + EPS), lax.rsqrt(s_hi * (1.0 / ATT_HD) + EPS))
    return y * inv


def _rope(y, cos, sin, lane):
    swap = jnp.where((lane & (ATT_HD // 2)) == 0,
                     pltpu.roll(y, LANES - ATT_HD // 2, 1), pltpu.roll(y, ATT_HD // 2, 1))
    return y * cos + swap * sin


def _qkprep_kernel(*refs, rope, emit_state):
    if rope:
        q_ref, k_ref, gq_ref, gk_ref, cos_ref, sin_ref = refs[:6]
        outs = refs[6:]
    else:
        q_ref, k_ref, gq_ref, gk_ref = refs[:4]
        outs = refs[4:]
    qo_ref, ko_ref = outs[0], outs[1]
    tm = q_ref.shape[0]
    lane = lax.broadcasted_iota(jnp.int32, (tm, LANES), 1)
    for h in range(N_HEADS):
        cs = slice(h * LANES, (h + 1) * LANES)
        q = _group_rms(q_ref[:, cs].astype(F32), lane) * gq_ref[...]
        k = _group_rms(k_ref[:, cs].astype(F32), lane) * gk_ref[...]
        if emit_state:
            outs[2][:, cs] = k
        if rope:
            q = _rope(q, cos_ref[...], sin_ref[...], lane)
            k = _rope(k, cos_ref[...], sin_ref[...], lane)
        qo_ref[:, cs] = (q * Q_SCALE).astype(BF16)
        ko_ref[:, cs] = k.astype(BF16)


def qk_prep(proj, q_gain, k_gain, rope_tabs, seq_len, emit_state, tm=512):
    t = proj.shape[0]
    tm = min(tm, seq_len)
    gq = jnp.tile(q_gain.reshape(1, ATT_HD), (1, 2))
    gk = jnp.tile(k_gain.reshape(1, ATT_HD), (1, 2))
    in_specs = [
        pl.BlockSpec((tm, ATT_WIDTH), lambda i: (i, 0)),
        pl.BlockSpec((tm, ATT_WIDTH), lambda i: (i, 1)),
        pl.BlockSpec((1, LANES), lambda i: (0, 0)),
        pl.BlockSpec((1, LANES), lambda i: (0, 0)),
    ]
    args = [proj, proj, gq, gk]
    if rope_tabs is not None:
        nblk = seq_len // tm
        in_specs += [pl.BlockSpec((tm, LANES), lambda i: (i % nblk, 0))] * 2
        args += list(rope_tabs)
    out_shape = [jax.ShapeDtypeStruct((t, ATT_WIDTH), BF16)] * 2
    out_specs = [pl.BlockSpec((tm, ATT_WIDTH), lambda i: (i, 0))] * 2
    if emit_state:
        out_shape.append(jax.ShapeDtypeStruct((t, ATT_WIDTH), F32))
        out_specs.append(pl.BlockSpec((tm, ATT_WIDTH), lambda i: (i, 0)))
    return pl.pallas_call(
        functools.partial(_qkprep_kernel, rope=rope_tabs is not None, emit_state=emit_state),
        grid=(t // tm,),
        in_specs=in_specs,
        out_specs=out_specs,
        out_shape=out_shape,
        compiler_params=_cparams("parallel"),
        name="qk_prep",
    )(*args)


def rope_tables(n_tokens):
    pos = jnp.arange(n_tokens)
    rows = (pos // GRID_W).astype(F32)
    cols = (pos % GRID_W).astype(F32)
    quarter = ATT_HD // 4
    inv = ROPE_THETA ** (-jnp.arange(quarter, dtype=F32) / quarter)
    ang = jnp.concatenate([rows[:, None] * inv, cols[:, None] * inv], -1)
    cos, sin = jnp.cos(ang), jnp.sin(ang)
    return jnp.tile(cos, (1, 4)), jnp.concatenate([-sin, sin, -sin, sin], -1)


def _attn_kernel(*refs, n_chunks, tk, has_ctx, lam_init):
    if has_ctx:
        q_ref, k_ref, v_ref, ck_ref, cv_ref, lw_ref, sub_ref, o_ref, s_ref = refs
    else:
        q_ref, k_ref, v_ref, lw_ref, sub_ref, o_ref, s_ref = refs
    tq = q_ref.shape[0]
    q = q_ref[...]
    lane = lax.broadcasted_iota(jnp.int32, (tq, LANES), 1)
    zero = jnp.zeros_like(q)
    qz = jnp.concatenate([jnp.where(lane < ATT_HD, q, zero), jnp.where(lane < ATT_HD, zero, q)], axis=0)

    def scores(kc):
        return lax.dot_general(qz, kc, (((1,), (1,)), ((), ())), preferred_element_type=F32)

    def fold_max(m, s):
        for t in range(s.shape[1] // LANES):
            m = jnp.maximum(m, s[:, t * LANES:(t + 1) * LANES])
        return m

    def weighted(s, vc):
        p = jnp.exp2(s - pltpu.repeat(mb, s.shape[1] // LANES, 1)).astype(BF16)
        return _dot(p, jnp.concatenate([vc, jnp.ones_like(vc)], axis=1))

    def pass1(c, mrun):
        off = pl.multiple_of(c * tk, tk)
        s = scores(k_ref[pl.ds(off, tk), :])
        s_ref[:, pl.ds(off, tk)] = s
        return fold_max(mrun, s)

    mrun = lax.fori_loop(0, n_chunks, pass1, jnp.full((2 * tq, LANES), -jnp.inf, F32))
    if has_ctx:
        s = scores(ck_ref[...].astype(BF16))
        s_ref[:, n_chunks * tk:] = s
        mrun = fold_max(mrun, s)
    mb = jnp.broadcast_to(jnp.max(mrun, axis=-1, keepdims=True), (2 * tq, LANES))

    def pass2(c, acc):
        off = pl.multiple_of(c * tk, tk)
        return acc + weighted(s_ref[:, pl.ds(off, tk)], v_ref[pl.ds(off, tk), :].astype(BF16))

    acc = lax.fori_loop(0, n_chunks, pass2, jnp.zeros((2 * tq, 2 * ATT_VD), F32))
    if has_ctx:
        acc = acc + weighted(s_ref[:, n_chunks * tk:], cv_ref[...].astype(BF16))
    acc, l = acc[:, :ATT_VD], acc[:, ATT_VD:]

    lw = lw_ref[...]
    lam = (jnp.exp(jnp.sum(lw[0:1] * lw[1:2], axis=-1, keepdims=True))
           - jnp.exp(jnp.sum(lw[2:3] * lw[3:4], axis=-1, keepdims=True)) + lam_init)
    o = acc[:tq] / l[:tq] - lam * (acc[tq:] / l[tq:])
    o = o * lax.rsqrt(jnp.mean(o * o, axis=-1, keepdims=True) + EPS) * sub_ref[...]
    o_ref[...] = (o * (1.0 - lam_init)).astype(o_ref.dtype)


def diff_attn(qn, kn, proj, cache_k, cache_v, layer, w_lambda, subln, lam_init, batch, seq_len,
              tq=512, tk=1024):
    t = qn.shape[0]
    tq = min(tq, seq_len)
    tk = min(tk, seq_len)
    nq = seq_len // tq
    n_chunks = seq_len // tk
    has_ctx = cache_k is not None
    v_col0 = 2 * ATT_WIDTH // LANES
    in_specs = [
        pl.BlockSpec((tq, LANES), lambda b, h, i: (b * nq + i, h)),
        pl.BlockSpec((seq_len, LANES), lambda b, h, i: (b, h)),
        pl.BlockSpec((seq_len, LANES), lambda b, h, i: (b, v_col0 + h)),
    ]
    args = [qn, kn, proj]
    if has_ctx:
        past = cache_k.shape[2]
        in_specs += [pl.BlockSpec((None, None, past, LANES), lambda b, h, i: (b, layer, 0, h))] * 2
        args += [cache_k, cache_v]
    in_specs += [pl.BlockSpec((4, ATT_HD), lambda b, h, i: (0, 0)),
                 pl.BlockSpec((1, ATT_VD), lambda b, h, i: (0, 0))]
    args += [w_lambda, subln.reshape(1, ATT_VD)]
    return pl.pallas_call(
        functools.partial(_attn_kernel, n_chunks=n_chunks, tk=tk, has_ctx=has_ctx, lam_init=lam_init),
        grid=(batch, N_HEADS, nq),
        in_specs=in_specs,
        out_specs=pl.BlockSpec((tq, LANES), lambda b, h, i: (b * nq + i, h)),
        out_shape=jax.ShapeDtypeStruct((t, ATT_WIDTH), BF16),
        scratch_shapes=[pltpu.VMEM((2 * tq, seq_len + (cache_k.shape[2] if has_ctx else 0)), F32)],
        compiler_params=_cparams("parallel", "parallel", "arbitrary"),
        name="diff_attn",
    )(*args)


def _shortconv_kernel(u_ref, prev_ref, next_ref, w_ref, b_ref, v_ref, x1_ref, x2_ref, *, tiles_per_seq):
    i = pl.program_id(0)
    tl = u_ref.shape[0]
    u = u_ref[...].astype(F32)
    first = (i % tiles_per_seq) == 0
    last = (i % tiles_per_seq) == tiles_per_seq - 1
    sub = prev_ref.shape[0]
    prev_row = jnp.where(first, 0.0, prev_ref[sub - 1:sub, :].astype(F32))
    next_row = jnp.where(last, 0.0, next_ref[0:1, :].astype(F32))
    row = lax.broadcasted_iota(jnp.int32, u.shape, 0)
    below = jnp.where(row == 0, prev_row, pltpu.roll(u, 1, 0))
    above = jnp.where(row == tl - 1, next_row, pltpu.roll(u, tl - 1, 0))
    w = w_ref[...]
    y = below * w[0:1] + u * w[1:2] + above * w[2:3] + b_ref[...]
    v_ref[...] = y[:, :HY_WIDTH]
    x1_ref[...] = y[:, HY_WIDTH:2 * HY_WIDTH]
    x2_ref[...] = y[:, 2 * HY_WIDTH:]


def short_conv(proj, w_sc, b_sc, seq_len, tl=256):
    t = proj.shape[0]
    width = 3 * HY_WIDTH
    cb = 3 * ATT_WIDTH // width
    tl = min(tl, seq_len)
    sub = 16
    nsub = tl // sub
    last_sub = t // sub - 1
    part = jax.ShapeDtypeStruct((t, HY_WIDTH), F32)
    return pl.pallas_call(
        functools.partial(_shortconv_kernel, tiles_per_seq=seq_len // tl),
        grid=(t // tl,),
        in_specs=[
            pl.BlockSpec((tl, width), lambda i: (i, cb)),
            pl.BlockSpec((sub, width), lambda i: (jnp.maximum(i * nsub - 1, 0), cb)),
            pl.BlockSpec((sub, width), lambda i: (jnp.minimum((i + 1) * nsub, last_sub), cb)),
            pl.BlockSpec((3, width), lambda i: (0, 0)),
            pl.BlockSpec((1, width), lambda i: (0, 0)),
        ],
        out_specs=[pl.BlockSpec((tl, HY_WIDTH), lambda i: (i, 0))] * 3,
        out_shape=[part] * 3,
        compiler_params=_cparams("parallel"),
        name="short_conv",
    )(proj, proj, proj, w_sc, b_sc.reshape(1, width))


def _filter_kernel(z_ref, w1_ref, b1_ref, f1_ref, w2_ref, b2_ref, f2_ref, w3f_ref, w3b_ref, dl_ref,
                   o_ref, h_ref, *, seq_len):
    @pl.when(pl.program_id(0) == 0)
    def _():
        h = jnp.sin(f1_ref[...] * (_dot3(z_ref[...], w1_ref[...]) + b1_ref[...]))
        h_ref[...] = jnp.sin(f2_ref[...] * (_dot3(h, w2_ref[...]) + b2_ref[...]))

    n = 2 * seq_len
    t = z_ref[:, 0:1]
    window = jnp.exp(-t * dl_ref[...]) + HY_DECAY_SHIFT
    fwd = _dot3(h_ref[0:seq_len, :], w3f_ref[...])
    bwd = _dot3(h_ref[seq_len:n, :], w3b_ref[...])
    row = lax.broadcasted_iota(jnp.int32, (seq_len, 1), 0)
    bwd = jnp.where(row == 0, 0.0, bwd)
    fwd = fwd * window[0:seq_len]
    bwd = bwd * window[seq_len:n]
    norm = jnp.sum(jnp.abs(fwd), axis=0, keepdims=True) + jnp.sum(jnp.abs(bwd), axis=0, keepdims=True)
    o_ref[0:seq_len, :] = fwd / norm
    o_ref[seq_len:n, :] = bwd / norm


def hyena_embedding(seq_len):
    pos = jnp.concatenate([jnp.arange(seq_len), seq_len - jnp.arange(seq_len)]) % seq_len
    t = jnp.linspace(0.0, 1.0, seq_len, dtype=F32)[pos][:, None]
    wpos = (2.0 * math.pi * jnp.arange(seq_len, dtype=F32) / seq_len)[pos][:, None]
    bands = jnp.linspace(1e-4, HY_BANDS - 1, HY_BANDS, dtype=F32)[None, :]
    z = jnp.concatenate([t, jnp.cos(bands * wpos), jnp.sin(bands * wpos)], -1)
    return jnp.pad(z, ((0, 0), (0, EMB_PAD - HY_EMB)))


def hyena_filter(seq_len, w1, b1, fr1, w2, b2, fr2, w3, tn=128):
    n = 2 * seq_len
    z = hyena_embedding(seq_len)
    w1p = jnp.pad(w1, ((0, EMB_PAD - HY_EMB), (0, 0)))
    deltas = jnp.abs(jnp.linspace(HY_MIN_DECAY, HY_MAX_DECAY, HY_WIDTH, dtype=F32)).reshape(1, HY_WIDTH)
    cpo = HY_WIDTH // tn
    nblk = HY_ORDER * cpo
    row = lambda a: a.reshape(1, -1)
    full = lambda shape: pl.BlockSpec(shape, lambda j: (0,) * len(shape))
    return pl.pallas_call(
        functools.partial(_filter_kernel, seq_len=seq_len),
        grid=(nblk,),
        in_specs=[
            full((n, EMB_PAD)), full((EMB_PAD, HY_HIDDEN)), full((1, HY_HIDDEN)), full((1, HY_HIDDEN)),
            full((HY_HIDDEN, HY_HIDDEN)), full((1, HY_HIDDEN)), full((1, HY_HIDDEN)),
            pl.BlockSpec((HY_HIDDEN, tn), lambda j: (0, j)),
            pl.BlockSpec((HY_HIDDEN, tn), lambda j: (0, nblk + j)),
            pl.BlockSpec((1, tn), lambda j: (0, j % cpo)),
        ],
        out_specs=pl.BlockSpec((None, n, tn), lambda j: (j // cpo, 0, j % cpo)),
        out_shape=jax.ShapeDtypeStruct((HY_ORDER, n, HY_WIDTH), F32),
        scratch_shapes=[pltpu.VMEM((n, HY_HIDDEN), F32)],
        compiler_params=_cparams("arbitrary"),
        name="hyena_filter",
    )(z, w1p, row(b1), row(fr1), w2, row(b2), row(fr2), w3, w3, deltas)


SUB = 8
STAGE_B_UNROLL = 4


def dft_split(seq_len):
    return (64, 128) if seq_len >= 4096 else (16, 2 * seq_len // 16)


def _angles(a, b, period):
    return (2.0 * math.pi / period) * ((a * b) % period).astype(F32)


def dft_tables(seq_len):
    n1, n2 = dft_split(seq_len)
    n = n1 * n2
    k1 = jnp.arange(n1)
    th = _angles(k1[:, None], k1[None, :], n1)
    c, s = jnp.cos(th), jnp.sin(th)
    fa = jnp.stack([c, -s], axis=1).reshape(2 * n1, n1)
    fc = jnp.stack([c, -s], axis=2).reshape(n1, 2 * n1)[: n1 // 2] / n
    eye = jnp.eye(SUB, dtype=F32)
    kk = k1[:, None, None] + n1 * jnp.arange(n2)[None, :, None]
    th = _angles(kk, jnp.arange(n2)[None, None, :], n)
    c, s = jnp.cos(th), jnp.sin(th)
    gf = jnp.concatenate([jnp.concatenate([c, s], 2), jnp.concatenate([-s, c], 2)], 1)
    ct, st = jnp.swapaxes(c, 1, 2), jnp.swapaxes(s, 1, 2)
    gi = jnp.concatenate([jnp.concatenate([ct, -st], 2), jnp.concatenate([st, ct], 2)], 1)
    return dict(n1=n1, n2=n2,
                fk_full=_split(jnp.kron(fa, eye)),
                fk=jnp.kron(fa[:, : n1 // 2], eye).astype(BF16),
                fki=jnp.kron(fc, eye).astype(BF16),
                gf=_split(gf), gi=gi.astype(BF16))


def _resident(shape):
    return pl.BlockSpec(shape, lambda *_: (0,) * len(shape), pipeline_mode=pl.Buffered(1))


def _spectrum_kernel(k_ref, fkh_ref, fkl_ref, gh_ref, gl_ref, o_ref, a_ref):
    rows, groups, _, tc = k_ref.shape
    n1 = gh_ref.shape[0]

    def stage_a(g, carry):
        xh, xl = _split(k_ref[:, g].reshape(rows * SUB, tc))
        r = _dot(fkh_ref[...], xh) + _dot(fkh_ref[...], xl) + _dot(fkl_ref[...], xh)
        a_ref[:, :, g] = r.reshape(n1, 2, SUB, tc)
        return carry

    lax.fori_loop(0, groups, stage_a, 0)

    def stage_b(k, carry):
        ah, al = _split(a_ref[k].reshape(2 * groups * SUB, tc))
        o_ref[k] = _dot(gh_ref[k], ah) + _dot(gh_ref[k], al) + _dot(gl_ref[k], ah)
        return carry

    lax.fori_loop(0, n1, stage_b, 0)


def filter_spectrum(kc, tabs, tc=128):
    n1, n2 = tabs["n1"], tabs["n2"]
    orders, n, c = kc.shape
    groups = n2 // SUB
    fkh, fkl = tabs["fk_full"]
    gh, gl = tabs["gf"]
    return pl.pallas_call(
        _spectrum_kernel,
        grid=(orders, c // tc),
        in_specs=[pl.BlockSpec((None, n1, groups, SUB, tc), lambda o, j: (o, 0, 0, 0, j)),
                  _resident(fkh.shape), _resident(fkl.shape), _resident(gh.shape), _resident(gl.shape)],
        out_specs=pl.BlockSpec((None, n1, 2 * n2, tc), lambda o, j: (o, 0, 0, j)),
        out_shape=jax.ShapeDtypeStruct((orders, n1, 2 * n2, c), F32),
        scratch_shapes=[pltpu.VMEM((n1, 2, groups, SUB, tc), F32)],
        compiler_params=_cparams("parallel", "arbitrary"),
        name="hyena_filter_spectrum",
    )(kc.reshape(orders, n1, groups, SUB, c), fkh, fkl, gh, gl)


def _hyena_conv_kernel(z_ref, x_ref, d_ref, kf_ref, fk_ref, fki_ref, gf_ref, gi_ref, o_ref, a_ref):
    rows, groups, _, tc = z_ref.shape
    n1 = gf_ref.shape[0]
    half = groups * SUB

    def stage_a(g, carry):
        xg = z_ref[:, g].reshape(rows * SUB, tc).astype(BF16)
        a_ref[:, :, g] = _dot(fk_ref[...], xg).reshape(n1, 2, SUB, tc)
        return carry

    lax.fori_loop(0, groups, stage_a, 0)

    def stage_b(kk, carry):
        for u in range(STAGE_B_UNROLL):
            k = kk * STAGE_B_UNROLL + u
            x = _dot(gf_ref[k], a_ref[k].reshape(2 * half, tc).astype(BF16))
            kf = kf_ref[k]
            xr, xi = x[:half], x[half:]
            kr, ki = kf[:half], kf[half:]
            p = jnp.concatenate([xr * kr - xi * ki, xr * ki + xi * kr], axis=0).astype(BF16)
            a_ref[k] = _dot(gi_ref[k], p).reshape(2, groups, SUB, tc)
        return carry

    lax.fori_loop(0, n1 // STAGE_B_UNROLL, stage_b, 0)

    d = d_ref[...].reshape(1, 1, tc)

    def stage_c(g, carry):
        bg = a_ref[:, :, g].reshape(n1 * 2 * SUB, tc).astype(BF16)
        y = _dot(fki_ref[...], bg).reshape(rows, SUB, tc)
        o_ref[:, g] = x_ref[:, g] * (y + z_ref[:, g] * d)
        return carry

    lax.fori_loop(0, groups, stage_c, 0)


def hyena_conv(z, x, d, kf, order, tabs, tc=128):
    n1, n2 = tabs["n1"], tabs["n2"]
    b, seq_len, c = z.shape
    rows, groups = n1 // 2, n2 // SUB
    view = lambda a: a.reshape(b, rows, groups, SUB, c)
    seq_spec = pl.BlockSpec((None, rows, groups, SUB, tc), lambda j, i: (i, 0, 0, 0, j))
    gf = tabs["gf"][0]
    out = pl.pallas_call(
        _hyena_conv_kernel,
        grid=(c // tc, b),
        in_specs=[seq_spec, seq_spec,
                  pl.BlockSpec((1, tc), lambda j, i: (0, j)),
                  pl.BlockSpec((None, n1, 2 * n2, tc), lambda j, i: (order, 0, 0, j), pipeline_mode=pl.Buffered(1)),
                  _resident(tabs["fk"].shape), _resident(tabs["fki"].shape),
                  _resident(gf.shape), _resident(tabs["gi"].shape)],
        out_specs=seq_spec,
        out_shape=jax.ShapeDtypeStruct((b, rows, groups, SUB, c), F32),
        scratch_shapes=[pltpu.VMEM((n1, 2, groups, SUB, tc), F32)],
        compiler_params=_cparams("parallel", "arbitrary"),
        name="hyena_conv",
    )(view(z), view(x), d.reshape(1, c), kf, tabs["fk"], tabs["fki"], gf, tabs["gi"])
    return out.reshape(b, seq_len, c)


def hyena_mixer(proj, batch, seq_len, tabs, w_sc, b_sc, w1, b1, fr1, w2, b2, fr2, w3, hy_d):
    c = HY_WIDTH
    kc = hyena_filter(seq_len, w1, b1, fr1, w2, b2, fr2, w3)
    kf = filter_spectrum(kc, tabs)
    v, x1, x2 = short_conv(proj, w_sc, b_sc, seq_len)
    z = v.reshape(batch, seq_len, c)
    for o, x in enumerate((x1, x2)):
        z = hyena_conv(z, x.reshape(batch, seq_len, c), hy_d[o], kf, o, tabs)
    return z.reshape(batch * seq_len, c)


def _pool_kernel(u_ref, w_ref, s_ref, o_ref, pad_ref):
    seq_len = u_ref.shape[0]
    h = POOL_HALO
    zeros = jnp.zeros((h, POOL_WIDTH), F32)
    pad_ref[0:h, :] = zeros
    pad_ref[h + seq_len:h + seq_len + h, :] = zeros
    pad_ref[h:h + seq_len, :] = u_ref[...].astype(F32)
    t = lax.broadcasted_iota(jnp.int32, (seq_len, 1), 0)
    for g, w in enumerate(POOL_WINDOWS):
        cs = slice(g * POOL_GW, (g + 1) * POOL_GW)
        tot = pad_ref[h - w // 2:h - w // 2 + seq_len, cs]
        for d in range(1 - w // 2, w - w // 2):
            tot = tot + pad_ref[h + d:h + d + seq_len, cs]
        cnt = (jnp.minimum(t - w // 2 + w, seq_len) - jnp.maximum(t - w // 2, 0)).astype(F32)
        pooled = tot / cnt - pad_ref[h:h + seq_len, cs]
        o_ref[:, cs] = (_dot(pooled.astype(BF16), w_ref[g]) * s_ref[:, cs]).astype(o_ref.dtype)


def pool_mixer(proj, w_pool, pool_scale, batch, seq_len):
    t = proj.shape[0]
    cb = (IN_COLS - POOL_WIDTH) // POOL_WIDTH
    return pl.pallas_call(
        _pool_kernel,
        grid=(batch,),
        in_specs=[pl.BlockSpec((seq_len, POOL_WIDTH), lambda b: (b, cb)),
                  pl.BlockSpec((len(POOL_WINDOWS), POOL_GW, POOL_GW), lambda b: (0, 0, 0)),
                  pl.BlockSpec((1, POOL_WIDTH), lambda b: (0, 0))],
        out_specs=pl.BlockSpec((seq_len, POOL_WIDTH), lambda b: (b, 0)),
        out_shape=jax.ShapeDtypeStruct((t, POOL_WIDTH), BF16),
        scratch_shapes=[pltpu.VMEM((seq_len + 2 * POOL_HALO, POOL_WIDTH), F32)],
        compiler_params=_cparams("parallel"),
        name="pool_mixer",
    )(proj, w_pool.astype(BF16), pool_scale.reshape(1, POOL_WIDTH))


def _outproj_kernel(att_ref, hy_ref, po_ref, x_ref, wa_ref, wh_ref, wp_ref, g1_ref, g_ref, sc_ref, sh_ref,
                    x1_ref, h2_ref):
    mix = (_dot(att_ref[...], wa_ref[...]) + _dot(hy_ref[...].astype(BF16), wh_ref[...])
           + _dot(po_ref[...], wp_ref[...]))
    x1 = x_ref[...] + g1_ref[...] * mix
    x1_ref[...] = x1
    y = x1 * lax.rsqrt(jnp.mean(x1 * x1, axis=-1, keepdims=True) + EPS) * g_ref[...]
    h2_ref[...] = (y * (1.0 + sc_ref[...]) + sh_ref[...]).astype(BF16)


def out_proj(att, hy, po, x, w_out, norm_g, mod, row_of_token, tm=512):
    t, d = x.shape
    a0, a1 = ATT_WIDTH, ATT_WIDTH + HY_WIDTH
    const = lambda shape: pl.BlockSpec(shape, lambda i: (0, 0))
    return pl.pallas_call(
        _outproj_kernel,
        grid=(t // tm,),
        in_specs=[
            pl.BlockSpec((tm, ATT_WIDTH), lambda i: (i, 0)),
            pl.BlockSpec((tm, HY_WIDTH), lambda i: (i, 0)),
            pl.BlockSpec((tm, POOL_WIDTH), lambda i: (i, 0)),
            pl.BlockSpec((tm, d), lambda i: (i, 0)),
            const((ATT_WIDTH, d)), const((HY_WIDTH, d)), const((POOL_WIDTH, d)),
            _mod_spec(2, row_of_token, tm),
            const((1, d)),
            _mod_spec(4, row_of_token, tm),
            _mod_spec(3, row_of_token, tm),
        ],
        out_specs=[pl.BlockSpec((tm, d), lambda i: (i, 0))] * 2,
        out_shape=[jax.ShapeDtypeStruct((t, d), F32), jax.ShapeDtypeStruct((t, d), BF16)],
        compiler_params=_cparams("parallel"),
        name="out_proj",
    )(att, hy, po, x, w_out[:a0], w_out[a0:a1], w_out[a1:], mod, norm_g.reshape(1, d), mod, mod)


def _ffn_kernel(h_ref, x_ref, wg_ref, wu_ref, wd_ref, g2_ref, o_ref, acc_ref):
    j = pl.program_id(1)

    @pl.when(j == 0)
    def _():
        acc_ref[...] = jnp.zeros_like(acc_ref)

    h = h_ref[...]
    gate = _dot(h, wg_ref[...])
    up = _dot(h, wu_ref[...])
    act = (gate * (1.0 / (1.0 + jnp.exp(-gate))) * up).astype(BF16)
    acc_ref[...] += _dot(act, wd_ref[...])

    @pl.when(j == pl.num_programs(1) - 1)
    def _():
        o_ref[...] = x_ref[...] + g2_ref[...] * acc_ref[...]


def ffn(h2, x1, w_gate, w_up, w_down, mod, row_of_token, tm=512, th=512):
    t, d = x1.shape
    hidden = w_gate.shape[1]
    return pl.pallas_call(
        _ffn_kernel,
        grid=(t // tm, hidden // th),
        in_specs=[
            pl.BlockSpec((tm, d), lambda i, j: (i, 0)),
            pl.BlockSpec((tm, d), lambda i, j: (i, 0)),
            pl.BlockSpec((d, th), lambda i, j: (0, j)),
            pl.BlockSpec((d, th), lambda i, j: (0, j)),
            pl.BlockSpec((th, d), lambda i, j: (j, 0)),
            _mod_spec(5, row_of_token, tm),
        ],
        out_specs=pl.BlockSpec((tm, d), lambda i, j: (i, 0)),
        out_shape=jax.ShapeDtypeStruct((t, d), F32),
        scratch_shapes=[pltpu.VMEM((tm, d), F32)],
        compiler_params=_cparams("parallel", "arbitrary"),
        name="ffn",
    )(h2, x1, w_gate, w_up, w_down, mod)


def trunk_layer(x, batch, seq_len, mod, row_of_token, rope_tabs, cache, layer, lam_init, tabs, p, emit_state):
    proj = in_proj(x, p["norm1"], mod, row_of_token, p["w_in"], F32 if emit_state else BF16)
    prep = qk_prep(proj, p["q_norm"], p["k_norm"], rope_tabs, seq_len, emit_state)
    cache_k, cache_v = cache if cache is not None else (None, None)
    att = diff_attn(prep[0], prep[1], proj, cache_k, cache_v, layer, p["w_lambda"], p["subln"], lam_init,
                    batch, seq_len)
    hy = hyena_mixer(proj, batch, seq_len, tabs, p["w_sc"], p["b_sc"], p["hy_w1"], p["hy_b1"], p["hy_fr1"],
                     p["hy_w2"], p["hy_b2"], p["hy_fr2"], p["hy_w3"], p["hy_d"])
    po = pool_mixer(proj, p["w_pool"], p["pool_scale"], batch, seq_len)
    x1, h2 = out_proj(att, hy, po, x, p["w_out"], p["norm2"], mod, row_of_token)
    y = ffn(h2, x1, p["w_gate"], p["w_up"], p["w_down"], mod, row_of_token)
    if emit_state:
        return y, prep[2], proj[:, 2 * ATT_WIDTH:3 * ATT_WIDTH]
    return y, None, None


def kernel(x_prompt, x_sample, cache_k, cache_v, c, c_ctx, w_ada, b_ada, norm1, norm2, w_in, q_norm, k_norm, w_lambda, subln, w_sc, b_sc, hy_w1, hy_b1, hy_fr1, hy_w2, hy_b2, hy_fr2, hy_w3, hy_d, w_pool, pool_scale, w_out, w_gate, w_up, w_down):
    batch, seq, d = x_prompt.shape
    dec_batch, dec_seq, _ = x_sample.shape
    depth = w_ada.shape[0]
    past = cache_k.shape[2]

    cond = jnp.concatenate([c_ctx[None, :], c, jnp.zeros((MOD_ROWS - 1 - dec_batch, d), F32)], axis=0)
    mod_all = ada_mod(cond, w_ada, b_ada)
    rope_tabs = rope_tables(dec_seq)
    tabs_ctx, tabs_lat = dft_tables(seq), dft_tables(dec_seq)
    ck = cache_k.reshape(dec_batch, depth, past, ATT_WIDTH)
    cv = cache_v.reshape(dec_batch, depth, past, ATT_WIDTH)

    y_p = x_prompt.reshape(batch * seq, d)
    y_s = x_sample.reshape(dec_batch * dec_seq, d)
    ks, vs = [], []
    for l in range(depth):
        p = dict(norm1=norm1[l], norm2=norm2[l], w_in=w_in[l].astype(BF16), q_norm=q_norm[l], k_norm=k_norm[l],
                 w_lambda=w_lambda[l], subln=subln[l], w_sc=w_sc[l], b_sc=b_sc[l], hy_w1=hy_w1[l],
                 hy_b1=hy_b1[l], hy_fr1=hy_fr1[l], hy_w2=hy_w2[l], hy_b2=hy_b2[l], hy_fr2=hy_fr2[l],
                 hy_w3=hy_w3[l], hy_d=hy_d[l], w_pool=w_pool[l], pool_scale=pool_scale[l],
                 w_out=w_out[l].astype(BF16), w_gate=w_gate[l].astype(BF16), w_up=w_up[l].astype(BF16),
                 w_down=w_down[l].astype(BF16))
        lam_init = 0.8 - 0.6 * math.exp(-0.3 * l)
        mod = mod_all[l].reshape(MOD_ROWS * 6, 1, d)
        y_p, k_l, v_l = trunk_layer(y_p, batch, seq, mod, lambda tok: 0, None, None, l, lam_init,
                                    tabs_ctx, p, True)
        ks.append(k_l.reshape(batch, seq, N_HEADS, 2, ATT_HD))
        vs.append(v_l.reshape(batch, seq, N_HEADS, ATT_VD))
        y_s, _, _ = trunk_layer(y_s, dec_batch, dec_seq, mod, lambda tok: 1 + tok // dec_seq, rope_tabs,
                                (ck, cv), l, lam_init, tabs_lat, p, False)
    return (y_p.reshape(batch, seq, d), y_s.reshape(dec_batch, dec_seq, d),
            jnp.stack(ks, axis=1), jnp.stack(vs, axis=1))
```

```python
import functools
import math

import jax
import jax.numpy as jnp
from jax import lax
from jax.experimental import pallas as pl
from jax.experimental.pallas import tpu as pltpu

F32 = jnp.float32
BF16 = jnp.bfloat16

D_MODEL = 2048
DEPTH = 2
GRID_W = 64
ATT_WIDTH = 1024
HY_WIDTH = 512
POOL_WIDTH = 512
ATT_HD = 64
ATT_VD = 128
N_HEADS = 8
ROPE_THETA = 10000.0
HY_ORDER = 2
HY_BANDS = 16
HY_EMB = 1 + 2 * HY_BANDS
HY_HIDDEN = 64
HY_DECAY_TARGET = 1e-2
HY_FAST_DECAY = 0.3
HY_SLOW_DECAY = 1.5
HY_DECAY_SHIFT = 0.05
HY_MIN_DECAY = math.log(HY_DECAY_TARGET) / HY_SLOW_DECAY
HY_MAX_DECAY = math.log(HY_DECAY_TARGET) / HY_FAST_DECAY
POOL_WINDOWS = (2, 4, 8, 16)
POOL_GW = 128
POOL_HALO = 16
FFN_HIDDEN = 5632
IN_COLS = 5120
EPS = 1e-6
Q_SCALE = ATT_HD ** -0.5 * math.log2(math.e)
MOD_ROWS = 16
EMB_PAD = 128
LANES = 128
VMEM_LIMIT = 56 * 1024 * 1024


def _cparams(*sem):
    return pltpu.CompilerParams(dimension_semantics=sem, vmem_limit_bytes=VMEM_LIMIT)


def _split(x):
    hi = x.astype(BF16)
    lo = (x - hi.astype(F32)).astype(BF16)
    return hi, lo


def _dot(a, b):
    return jnp.dot(a, b, preferred_element_type=F32)


def _dot3(a, b):
    ah, al = _split(a)
    bh, bl = _split(b)
    return _dot(ah, bh) + _dot(al, bh) + _dot(ah, bl)


def _ada_kernel(c_ref, w_ref, b_ref, o_ref):
    c = c_ref[...]
    a = c * (1.0 / (1.0 + jnp.exp(-c)))
    o_ref[...] = _dot3(a, w_ref[...]) + b_ref[...]


def ada_mod(cond, w_ada, b_ada, tn=1024):
    depth, d, n = w_ada.shape
    rows = cond.shape[0]
    return pl.pallas_call(
        _ada_kernel,
        grid=(depth, n // tn),
        in_specs=[
            pl.BlockSpec((rows, d), lambda l, j: (0, 0)),
            pl.BlockSpec((None, d, tn), lambda l, j: (l, 0, j)),
            pl.BlockSpec((None, 1, tn), lambda l, j: (l, 0, j)),
        ],
        out_specs=pl.BlockSpec((None, rows, tn), lambda l, j: (l, 0, j)),
        out_shape=jax.ShapeDtypeStruct((depth, rows, n), F32),
        compiler_params=_cparams("parallel", "parallel"),
        name="ada_mod",
    )(cond, w_ada, b_ada.reshape(depth, 1, n))


def _mod_spec(part, row_of_token, tm):
    return pl.BlockSpec((None, 1, D_MODEL), lambda i, *_: (row_of_token(i * tm) * 6 + part, 0, 0))


def _inproj_kernel(x_ref, g_ref, sc_ref, sh_ref, w_ref, o_ref, h_ref):
    @pl.when(pl.program_id(1) == 0)
    def _():
        x = x_ref[...]
        y = x * lax.rsqrt(jnp.mean(x * x, axis=-1, keepdims=True) + EPS) * g_ref[...]
        h_ref[...] = (y * (1.0 + sc_ref[...]) + sh_ref[...]).astype(BF16)

    o_ref[...] = _dot(h_ref[...], w_ref[...]).astype(o_ref.dtype)


def in_proj(x, norm_g, mod, row_of_token, w_in, out_dtype, tm=1024, tn=1024):
    t, d = x.shape
    n = w_in.shape[1]
    tm = min(tm, t)
    return pl.pallas_call(
        _inproj_kernel,
        grid=(t // tm, n // tn),
        in_specs=[
            pl.BlockSpec((tm, d), lambda i, j: (i, 0)),
            pl.BlockSpec((1, d), lambda i, j: (0, 0)),
            _mod_spec(1, row_of_token, tm),
            _mod_spec(0, row_of_token, tm),
            pl.BlockSpec((d, tn), lambda i, j: (0, j)),
        ],
        out_specs=pl.BlockSpec((tm, tn), lambda i, j: (i, j)),
        out_shape=jax.ShapeDtypeStruct((t, n), out_dtype),
        scratch_shapes=[pltpu.VMEM((tm, d), BF16)],
        compiler_params=_cparams("parallel", "arbitrary"),
        name="in_proj",
    )(x, norm_g.reshape(1, d), mod, mod, w_in)


def _group_rms(y, lane):
    ss = y * y
    lo = lane < ATT_HD
    s_lo = jnp.sum(jnp.where(lo, ss, 0.0), axis=-1, keepdims=True)
    s_hi = jnp.sum(jnp.where(lo, 0.0, ss), axis=-1, keepdims=True)
    inv = jnp.where(lo, lax.rsqrt(s_lo * (1.0 / ATT_HD) + EPS), lax.rsqrt(s_hi * (1.0 / ATT_HD) + EPS))
    return y * inv


def _rope(y, cos, sin, lane):
    swap = jnp.where((lane & (ATT_HD // 2)) == 0,
                     pltpu.roll(y, LANES - ATT_HD // 2, 1), pltpu.roll(y, ATT_HD // 2, 1))
    return y * cos + swap * sin


def _qkprep_kernel(*refs, rope, emit_state):
    if rope:
        q_ref, k_ref, gq_ref, gk_ref, cos_ref, sin_ref = refs[:6]
        outs = refs[6:]
    else:
        q_ref, k_ref, gq_ref, gk_ref = refs[:4]
        outs = refs[4:]
    qo_ref, ko_ref = outs[0], outs[1]
    tm = q_ref.shape[0]
    lane = lax.broadcasted_iota(jnp.int32, (tm, LANES), 1)
    for h in range(N_HEADS):
        cs = slice(h * LANES, (h + 1) * LANES)
        q = _group_rms(q_ref[:, cs].astype(F32), lane) * gq_ref[...]
        k = _group_rms(k_ref[:, cs].astype(F32), lane) * gk_ref[...]
        if emit_state:
            outs[2][:, cs] = k
        if rope:
            q = _rope(q, cos_ref[...], sin_ref[...], lane)
            k = _rope(k, cos_ref[...], sin_ref[...], lane)
        qo_ref[:, cs] = (q * Q_SCALE).astype(BF16)
        ko_ref[:, cs] = k.astype(BF16)


def qk_prep(proj, q_gain, k_gain, rope_tabs, seq_len, emit_state, tm=512):
    t = proj.shape[0]
    tm = min(tm, seq_len)
    gq = jnp.tile(q_gain.reshape(1, ATT_HD), (1, 2))
    gk = jnp.tile(k_gain.reshape(1, ATT_HD), (1, 2))
    in_specs = [
        pl.BlockSpec((tm, ATT_WIDTH), lambda i: (i, 0)),
        pl.BlockSpec((tm, ATT_WIDTH), lambda i: (i, 1)),
        pl.BlockSpec((1, LANES), lambda i: (0, 0)),
        pl.BlockSpec((1, LANES), lambda i: (0, 0)),
    ]
    args = [proj, proj, gq, gk]
    if rope_tabs is not None:
        nblk = seq_len // tm
        in_specs += [pl.BlockSpec((tm, LANES), lambda i: (i % nblk, 0))] * 2
        args += list(rope_tabs)
    out_shape = [jax.ShapeDtypeStruct((t, ATT_WIDTH), BF16)] * 2
    out_specs = [pl.BlockSpec((tm, ATT_WIDTH), lambda i: (i, 0))] * 2
    if emit_state:
        out_shape.append(jax.ShapeDtypeStruct((t, ATT_WIDTH), F32))
        out_specs.append(pl.BlockSpec((tm, ATT_WIDTH), lambda i: (i, 0)))
    return pl.pallas_call(
        functools.partial(_qkprep_kernel, rope=rope_tabs is not None, emit_state=emit_state),
        grid=(t // tm,),
        in_specs=in_specs,
        out_specs=out_specs,
        out_shape=out_shape,
        compiler_params=_cparams("parallel"),
        name="qk_prep",
    )(*args)


def rope_tables(n_tokens):
    pos = jnp.arange(n_tokens)
    rows = (pos // GRID_W).astype(F32)
    cols = (pos % GRID_W).astype(F32)
    quarter = ATT_HD // 4
    inv = ROPE_THETA ** (-jnp.arange(quarter, dtype=F32) / quarter)
    ang = jnp.concatenate([rows[:, None] * inv, cols[:, None] * inv], -1)
    cos, sin = jnp.cos(ang), jnp.sin(ang)
    return jnp.tile(cos, (1, 4)), jnp.concatenate([-sin, sin, -sin, sin], -1)


def _attn_kernel(*refs, n_chunks, tk, has_ctx, lam_init):
    if has_ctx:
        q_ref, k_ref, v_ref, ck_ref, cv_ref, lw_ref, sub_ref, o_ref, s_ref, ka_ref, va_ref = refs
    else:
        q_ref, k_ref, v_ref, lw_ref, sub_ref, o_ref, s_ref, ka_ref, va_ref = refs
    tq = q_ref.shape[0]
    seq = k_ref.shape[0]

    @pl.when(pl.program_id(2) == 0)
    def _():
        ka_ref[0:seq, :] = k_ref[...]
        va_ref[0:seq, :] = v_ref[...].astype(BF16)
        if has_ctx:
            ka_ref[seq:, :] = ck_ref[...].astype(BF16)
            va_ref[seq:, :] = cv_ref[...].astype(BF16)

    q = q_ref[...]
    lane = lax.broadcasted_iota(jnp.int32, (tq, LANES), 1)
    zero = jnp.zeros_like(q)
    qz = jnp.concatenate([jnp.where(lane < ATT_HD, q, zero), jnp.where(lane < ATT_HD, zero, q)], axis=0)

    def scores(kc):
        return lax.dot_general(qz, kc, (((1,), (1,)), ((), ())), preferred_element_type=F32)

    def fold_max(m, s):
        for t in range(s.shape[1] // LANES):
            m = jnp.maximum(m, s[:, t * LANES:(t + 1) * LANES])
        return m

    def weighted(s, vc):
        p = jnp.exp2(s - jnp.concatenate([mb] * (s.shape[1] // LANES), axis=1)).astype(BF16)
        return _dot(p, jnp.concatenate([vc, jnp.ones_like(vc)], axis=1))

    def pass1(c, mrun):
        off = pl.multiple_of(c * tk, LANES)
        s = scores(ka_ref[pl.ds(off, tk), :])
        s_ref[:, pl.ds(off, tk)] = s
        return fold_max(mrun, s)

    mrun = lax.fori_loop(0, n_chunks, pass1, jnp.full((2 * tq, LANES), -jnp.inf, F32))
    mb = jnp.broadcast_to(jnp.max(mrun, axis=-1, keepdims=True), (2 * tq, LANES))

    def pass2(c, acc):
        off = pl.multiple_of(c * tk, LANES)
        return acc + weighted(s_ref[:, pl.ds(off, tk)], va_ref[pl.ds(off, tk), :])

    acc = lax.fori_loop(0, n_chunks, pass2, jnp.zeros((2 * tq, 2 * ATT_VD), F32))
    acc, l = acc[:, :ATT_VD], acc[:, ATT_VD:]

    lw = lw_ref[...]
    lam = (jnp.exp(jnp.sum(lw[0:1] * lw[1:2], axis=-1, keepdims=True))
           - jnp.exp(jnp.sum(lw[2:3] * lw[3:4], axis=-1, keepdims=True)) + lam_init)
    o = acc[:tq] / l[:tq] - lam * (acc[tq:] / l[tq:])
    o = o * lax.rsqrt(jnp.mean(o * o, axis=-1, keepdims=True) + EPS) * sub_ref[...]
    o_ref[...] = (o * (1.0 - lam_init)).astype(o_ref.dtype)


def diff_attn(qn, kn, proj, cache_k, cache_v, layer, w_lambda, subln, lam_init, batch, seq_len,
              tq=512, max_tk=2304):
    t = qn.shape[0]
    tq = min(tq, seq_len)
    nq = seq_len // tq
    has_ctx = cache_k is not None
    total = seq_len + (cache_k.shape[2] if has_ctx else 0)
    n_chunks = -(-total // max_tk)
    tk = total // n_chunks
    assert tk * n_chunks == total and tk % LANES == 0
    v_col0 = 2 * ATT_WIDTH // LANES
    in_specs = [
        pl.BlockSpec((tq, LANES), lambda b, h, i: (b * nq + i, h)),
        pl.BlockSpec((seq_len, LANES), lambda b, h, i: (b, h)),
        pl.BlockSpec((seq_len, LANES), lambda b, h, i: (b, v_col0 + h)),
    ]
    args = [qn, kn, proj]
    if has_ctx:
        past = cache_k.shape[2]
        in_specs += [pl.BlockSpec((None, None, past, LANES), lambda b, h, i: (b, layer, 0, h))] * 2
        args += [cache_k, cache_v]
    in_specs += [pl.BlockSpec((4, ATT_HD), lambda b, h, i: (0, 0)),
                 pl.BlockSpec((1, ATT_VD), lambda b, h, i: (0, 0))]
    args += [w_lambda, subln.reshape(1, ATT_VD)]
    return pl.pallas_call(
        functools.partial(_attn_kernel, n_chunks=n_chunks, tk=tk, has_ctx=has_ctx, lam_init=lam_init),
        grid=(batch, N_HEADS, nq),
        in_specs=in_specs,
        out_specs=pl.BlockSpec((tq, LANES), lambda b, h, i: (b * nq + i, h)),
        out_shape=jax.ShapeDtypeStruct((t, ATT_WIDTH), BF16),
        scratch_shapes=[pltpu.VMEM((2 * tq, total), F32),
                        pltpu.VMEM((total, LANES), BF16), pltpu.VMEM((total, LANES), BF16)],
        compiler_params=_cparams("parallel", "parallel", "arbitrary"),
        name="diff_attn",
    )(*args)


def _shortconv_kernel(u_ref, prev_ref, next_ref, w_ref, b_ref, v_ref, x1_ref, x2_ref, *, tiles_per_seq):
    i = pl.program_id(0)
    tl = u_ref.shape[0]
    u = u_ref[...].astype(F32)
    first = (i % tiles_per_seq) == 0
    last = (i % tiles_per_seq) == tiles_per_seq - 1
    sub = prev_ref.shape[0]
    prev_row = jnp.where(first, 0.0, prev_ref[sub - 1:sub, :].astype(F32))
    next_row = jnp.where(last, 0.0, next_ref[0:1, :].astype(F32))
    row = lax.broadcasted_iota(jnp.int32, u.shape, 0)
    below = jnp.where(row == 0, prev_row, pltpu.roll(u, 1, 0))
    above = jnp.where(row == tl - 1, next_row, pltpu.roll(u, tl - 1, 0))
    w = w_ref[...]
    y = below * w[0:1] + u * w[1:2] + above * w[2:3] + b_ref[...]
    v_ref[...] = y[:, :HY_WIDTH]
    x1_ref[...] = y[:, HY_WIDTH:2 * HY_WIDTH]
    x2_ref[...] = y[:, 2 * HY_WIDTH:]


def short_conv(proj, w_sc, b_sc, seq_len, tl=256):
    t = proj.shape[0]
    width = 3 * HY_WIDTH
    cb = 3 * ATT_WIDTH // width
    tl = min(tl, seq_len)
    sub = 16
    nsub = tl // sub
    last_sub = t // sub - 1
    part = jax.ShapeDtypeStruct((t, HY_WIDTH), F32)
    return pl.pallas_call(
        functools.partial(_shortconv_kernel, tiles_per_seq=seq_len // tl),
        grid=(t // tl,),
        in_specs=[
            pl.BlockSpec((tl, width), lambda i: (i, cb)),
            pl.BlockSpec((sub, width), lambda i: (jnp.maximum(i * nsub - 1, 0), cb)),
            pl.BlockSpec((sub, width), lambda i: (jnp.minimum((i + 1) * nsub, last_sub), cb)),
            pl.BlockSpec((3, width), lambda i: (0, 0)),
            pl.BlockSpec((1, width), lambda i: (0, 0)),
        ],
        out_specs=[pl.BlockSpec((tl, HY_WIDTH), lambda i: (i, 0))] * 3,
        out_shape=[part] * 3,
        compiler_params=_cparams("parallel"),
        name="short_conv",
    )(proj, proj, proj, w_sc, b_sc.reshape(1, width))


def _filter_kernel(z_ref, w1_ref, b1_ref, f1_ref, w2_ref, b2_ref, f2_ref, w3f_ref, w3b_ref, dl_ref,
                   o_ref, h_ref, *, seq_len):
    @pl.when(pl.program_id(0) == 0)
    def _():
        h = jnp.sin(f1_ref[...] * (_dot3(z_ref[...], w1_ref[...]) + b1_ref[...]))
        h_ref[...] = jnp.sin(f2_ref[...] * (_dot3(h, w2_ref[...]) + b2_ref[...]))

    n = 2 * seq_len
    t = z_ref[:, 0:1]
    window = jnp.exp(-t * dl_ref[...]) + HY_DECAY_SHIFT
    fwd = _dot3(h_ref[0:seq_len, :], w3f_ref[...])
    bwd = _dot3(h_ref[seq_len:n, :], w3b_ref[...])
    row = lax.broadcasted_iota(jnp.int32, (seq_len, 1), 0)
    bwd = jnp.where(row == 0, 0.0, bwd)
    fwd = fwd * window[0:seq_len]
    bwd = bwd * window[seq_len:n]
    norm = jnp.sum(jnp.abs(fwd), axis=0, keepdims=True) + jnp.sum(jnp.abs(bwd), axis=0, keepdims=True)
    o_ref[0:seq_len, :] = fwd / norm
    o_ref[seq_len:n, :] = bwd / norm


def hyena_embedding(seq_len):
    pos = jnp.concatenate([jnp.arange(seq_len), seq_len - jnp.arange(seq_len)]) % seq_len
    t = jnp.linspace(0.0, 1.0, seq_len, dtype=F32)[pos][:, None]
    wpos = (2.0 * math.pi * jnp.arange(seq_len, dtype=F32) / seq_len)[pos][:, None]
    bands = jnp.linspace(1e-4, HY_BANDS - 1, HY_BANDS, dtype=F32)[None, :]
    z = jnp.concatenate([t, jnp.cos(bands * wpos), jnp.sin(bands * wpos)], -1)
    return jnp.pad(z, ((0, 0), (0, EMB_PAD - HY_EMB)))


def hyena_filter(seq_len, w1, b1, fr1, w2, b2, fr2, w3, tn=128):
    n = 2 * seq_len
    z = hyena_embedding(seq_len)
    w1p = jnp.pad(w1, ((0, EMB_PAD - HY_EMB), (0, 0)))
    deltas = jnp.abs(jnp.linspace(HY_MIN_DECAY, HY_MAX_DECAY, HY_WIDTH, dtype=F32)).reshape(1, HY_WIDTH)
    cpo = HY_WIDTH // tn
    nblk = HY_ORDER * cpo
    row = lambda a: a.reshape(1, -1)
    full = lambda shape: pl.BlockSpec(shape, lambda j: (0,) * len(shape))
    return pl.pallas_call(
        functools.partial(_filter_kernel, seq_len=seq_len),
        grid=(nblk,),
        in_specs=[
            full((n, EMB_PAD)), full((EMB_PAD, HY_HIDDEN)), full((1, HY_HIDDEN)), full((1, HY_HIDDEN)),
            full((HY_HIDDEN, HY_HIDDEN)), full((1, HY_HIDDEN)), full((1, HY_HIDDEN)),
            pl.BlockSpec((HY_HIDDEN, tn), lambda j: (0, j)),
            pl.BlockSpec((HY_HIDDEN, tn), lambda j: (0, nblk + j)),
            pl.BlockSpec((1, tn), lambda j: (0, j % cpo)),
        ],
        out_specs=pl.BlockSpec((None, n, tn), lambda j: (j // cpo, 0, j % cpo)),
        out_shape=jax.ShapeDtypeStruct((HY_ORDER, n, HY_WIDTH), F32),
        scratch_shapes=[pltpu.VMEM((n, HY_HIDDEN), F32)],
        compiler_params=_cparams("arbitrary"),
        name="hyena_filter",
    )(z, w1p, row(b1), row(fr1), w2, row(b2), row(fr2), w3, w3, deltas)


SUB = 8
STAGE_B_UNROLL = 8
STAGE_AC_UNROLL = 2


def dft_split(seq_len):
    return (64, 128) if seq_len >= 4096 else (16, 2 * seq_len // 16)


def _angles(a, b, period):
    return (2.0 * math.pi / period) * ((a * b) % period).astype(F32)


def dft_tables(seq_len):
    n1, n2 = dft_split(seq_len)
    n = n1 * n2
    k1 = jnp.arange(n1)
    th = _angles(k1[:, None], k1[None, :], n1)
    c, s = jnp.cos(th), jnp.sin(th)
    fa = jnp.stack([c, -s], axis=1).reshape(2 * n1, n1)
    fc = jnp.stack([c, -s], axis=2).reshape(n1, 2 * n1)[: n1 // 2] / n
    eye = jnp.eye(SUB, dtype=F32)
    kk = k1[:, None, None] + n1 * jnp.arange(n2)[None, :, None]
    th = _angles(kk, jnp.arange(n2)[None, None, :], n)
    c, s = jnp.cos(th), jnp.sin(th)
    gf = jnp.concatenate([jnp.concatenate([c, s], 2), jnp.concatenate([-s, c], 2)], 1)
    ct, st = jnp.swapaxes(c, 1, 2), jnp.swapaxes(s, 1, 2)
    gi = jnp.concatenate([jnp.concatenate([ct, -st], 2), jnp.concatenate([st, ct], 2)], 1)
    return dict(n1=n1, n2=n2,
                fk_full=_split(jnp.kron(fa, eye)),
                fk=jnp.kron(fa[:, : n1 // 2], eye).astype(BF16),
                fki=jnp.kron(fc, eye).astype(BF16),
                gf=_split(gf), gi=gi.astype(BF16))


def _resident(shape):
    return pl.BlockSpec(shape, lambda *_: (0,) * len(shape), pipeline_mode=pl.Buffered(1))


def _spectrum_kernel(k_ref, fkh_ref, fkl_ref, gh_ref, gl_ref, o_ref, a_ref):
    rows, groups, _, tc = k_ref.shape
    n1 = gh_ref.shape[0]

    def stage_a(g, carry):
        xh, xl = _split(k_ref[:, g].reshape(rows * SUB, tc))
        r = _dot(fkh_ref[...], xh) + _dot(fkh_ref[...], xl) + _dot(fkl_ref[...], xh)
        a_ref[:, :, g] = r.reshape(n1, 2, SUB, tc)
        return carry

    lax.fori_loop(0, groups, stage_a, 0)

    def stage_b(kk, carry):
        for u in range(STAGE_B_UNROLL):
            k = kk * STAGE_B_UNROLL + u
            ah, al = _split(a_ref[k].reshape(2 * groups * SUB, tc))
            o_ref[k] = _dot(gh_ref[k], ah) + _dot(gh_ref[k], al) + _dot(gl_ref[k], ah)
        return carry

    lax.fori_loop(0, n1 // STAGE_B_UNROLL, stage_b, 0)


def filter_spectrum(kc, tabs, tc):
    n1, n2 = tabs["n1"], tabs["n2"]
    orders, n, c = kc.shape
    groups = n2 // SUB
    fkh, fkl = tabs["fk_full"]
    gh, gl = tabs["gf"]
    return pl.pallas_call(
        _spectrum_kernel,
        grid=(orders, c // tc),
        in_specs=[pl.BlockSpec((None, n1, groups, SUB, tc), lambda o, j: (o, 0, 0, 0, j)),
                  _resident(fkh.shape), _resident(fkl.shape), _resident(gh.shape), _resident(gl.shape)],
        out_specs=pl.BlockSpec((None, n1, 2 * n2, tc), lambda o, j: (o, 0, 0, j)),
        out_shape=jax.ShapeDtypeStruct((orders, n1, 2 * n2, c), F32),
        scratch_shapes=[pltpu.VMEM((n1, 2, groups, SUB, tc), F32)],
        compiler_params=_cparams("parallel", "arbitrary"),
        name="hyena_filter_spectrum",
    )(kc.reshape(orders, n1, groups, SUB, c), fkh, fkl, gh, gl)


def _hyena_conv_kernel(z_ref, x_ref, d_ref, kf_ref, fk_ref, fki_ref, gf_ref, gi_ref, o_ref, a_ref):
    rows, groups, _, tc = z_ref.shape
    n1 = gf_ref.shape[0]
    half = groups * SUB

    def stage_a(gg, carry):
        for u in range(STAGE_AC_UNROLL):
            g = gg * STAGE_AC_UNROLL + u
            xg = z_ref[:, g].reshape(rows * SUB, tc).astype(BF16)
            a_ref[:, :, g] = _dot(fk_ref[...], xg).reshape(n1, 2, SUB, tc)
        return carry

    lax.fori_loop(0, groups // STAGE_AC_UNROLL, stage_a, 0)

    def stage_b(kk, carry):
        for u in range(STAGE_B_UNROLL):
            k = kk * STAGE_B_UNROLL + u
            x = _dot(gf_ref[k], a_ref[k].reshape(2 * half, tc).astype(BF16))
            kf = kf_ref[k]
            xr, xi = x[:half], x[half:]
            kr, ki = kf[:half], kf[half:]
            p = jnp.concatenate([xr * kr - xi * ki, xr * ki + xi * kr], axis=0).astype(BF16)
            a_ref[k] = _dot(gi_ref[k], p).reshape(2, groups, SUB, tc)
        return carry

    lax.fori_loop(0, n1 // STAGE_B_UNROLL, stage_b, 0)

    d = d_ref[...].reshape(1, 1, tc)

    def stage_c(gg, carry):
        for u in range(STAGE_AC_UNROLL):
            g = gg * STAGE_AC_UNROLL + u
            bg = a_ref[:, :, g].reshape(n1 * 2 * SUB, tc).astype(BF16)
            y = _dot(fki_ref[...], bg).reshape(rows, SUB, tc)
            o_ref[:, g] = x_ref[:, g] * (y + z_ref[:, g] * d)
        return carry

    lax.fori_loop(0, groups // STAGE_AC_UNROLL, stage_c, 0)


def hyena_conv(z, x, d, kf, order, tabs, tc):
    n1, n2 = tabs["n1"], tabs["n2"]
    b, seq_len, c = z.shape
    rows, groups = n1 // 2, n2 // SUB
    view = lambda a: a.reshape(b, rows, groups, SUB, c)
    seq_spec = pl.BlockSpec((None, rows, groups, SUB, tc), lambda j, i: (i, 0, 0, 0, j))
    gf = tabs["gf"][0]
    out = pl.pallas_call(
        _hyena_conv_kernel,
        grid=(c // tc, b),
        in_specs=[seq_spec, seq_spec,
                  pl.BlockSpec((1, tc), lambda j, i: (0, j)),
                  pl.BlockSpec((None, n1, 2 * n2, tc), lambda j, i: (order, 0, 0, j), pipeline_mode=pl.Buffered(1)),
                  _resident(tabs["fk"].shape), _resident(tabs["fki"].shape),
                  _resident(gf.shape), _resident(tabs["gi"].shape)],
        out_specs=seq_spec,
        out_shape=jax.ShapeDtypeStruct((b, rows, groups, SUB, c), F32),
        scratch_shapes=[pltpu.VMEM((n1, 2, groups, SUB, tc), F32)],
        compiler_params=_cparams("parallel", "arbitrary"),
        name="hyena_conv",
    )(view(z), view(x), d.reshape(1, c), kf, tabs["fk"], tabs["fki"], gf, tabs["gi"])
    return out.reshape(b, seq_len, c)


def hyena_mixer(proj, batch, seq_len, tabs, w_sc, b_sc, w1, b1, fr1, w2, b2, fr2, w3, hy_d):
    c = HY_WIDTH
    kc = hyena_filter(seq_len, w1, b1, fr1, w2, b2, fr2, w3)
    tc = LANES if seq_len >= 4096 else c
    kf = filter_spectrum(kc, tabs, tc)
    v, x1, x2 = short_conv(proj, w_sc, b_sc, seq_len)
    z = v.reshape(batch, seq_len, c)
    for o, x in enumerate((x1, x2)):
        z = hyena_conv(z, x.reshape(batch, seq_len, c), hy_d[o], kf, o, tabs, tc)
    return z.reshape(batch * seq_len, c)


def _pool_kernel(u_ref, w_ref, s_ref, o_ref, pad_ref):
    seq_len = u_ref.shape[0]
    h = POOL_HALO
    zeros = jnp.zeros((h, POOL_WIDTH), F32)
    pad_ref[0:h, :] = zeros
    pad_ref[h + seq_len:h + seq_len + h, :] = zeros
    pad_ref[h:h + seq_len, :] = u_ref[...].astype(F32)
    t = lax.broadcasted_iota(jnp.int32, (seq_len, 1), 0)
    for g, w in enumerate(POOL_WINDOWS):
        cs = slice(g * POOL_GW, (g + 1) * POOL_GW)
        tot = pad_ref[h - w // 2:h - w // 2 + seq_len, cs]
        for d in range(1 - w // 2, w - w // 2):
            tot = tot + pad_ref[h + d:h + d + seq_len, cs]
        cnt = (jnp.minimum(t - w // 2 + w, seq_len) - jnp.maximum(t - w // 2, 0)).astype(F32)
        pooled = tot / cnt - pad_ref[h:h + seq_len, cs]
        o_ref[:, cs] = (_dot(pooled.astype(BF16), w_ref[g]) * s_ref[:, cs]).astype(o_ref.dtype)


def pool_mixer(proj, w_pool, pool_scale, batch, seq_len):
    t = proj.shape[0]
    cb = (IN_COLS - POOL_WIDTH) // POOL_WIDTH
    return pl.pallas_call(
        _pool_kernel,
        grid=(batch,),
        in_specs=[pl.BlockSpec((seq_len, POOL_WIDTH), lambda b: (b, cb)),
                  pl.BlockSpec((len(POOL_WINDOWS), POOL_GW, POOL_GW), lambda b: (0, 0, 0)),
                  pl.BlockSpec((1, POOL_WIDTH), lambda b: (0, 0))],
        out_specs=pl.BlockSpec((seq_len, POOL_WIDTH), lambda b: (b, 0)),
        out_shape=jax.ShapeDtypeStruct((t, POOL_WIDTH), BF16),
        scratch_shapes=[pltpu.VMEM((seq_len + 2 * POOL_HALO, POOL_WIDTH), F32)],
        compiler_params=_cparams("parallel"),
        name="pool_mixer",
    )(proj, w_pool.astype(BF16), pool_scale.reshape(1, POOL_WIDTH))


def _outproj_kernel(att_ref, hy_ref, po_ref, x_ref, wa_ref, wh_ref, wp_ref, g1_ref, g_ref, sc_ref, sh_ref,
                    x1_ref, h2_ref):
    mix = (_dot(att_ref[...], wa_ref[...]) + _dot(hy_ref[...].astype(BF16), wh_ref[...])
           + _dot(po_ref[...], wp_ref[...]))
    x1 = x_ref[...] + g1_ref[...] * mix
    x1_ref[...] = x1
    y = x1 * lax.rsqrt(jnp.mean(x1 * x1, axis=-1, keepdims=True) + EPS) * g_ref[...]
    h2_ref[...] = (y * (1.0 + sc_ref[...]) + sh_ref[...]).astype(BF16)


def out_proj(att, hy, po, x, w_out, norm_g, mod, row_of_token, tm=512):
    t, d = x.shape
    a0, a1 = ATT_WIDTH, ATT_WIDTH + HY_WIDTH
    const = lambda shape: pl.BlockSpec(shape, lambda i: (0, 0))
    return pl.pallas_call(
        _outproj_kernel,
        grid=(t // tm,),
        in_specs=[
            pl.BlockSpec((tm, ATT_WIDTH), lambda i: (i, 0)),
            pl.BlockSpec((tm, HY_WIDTH), lambda i: (i, 0)),
            pl.BlockSpec((tm, POOL_WIDTH), lambda i: (i, 0)),
            pl.BlockSpec((tm, d), lambda i: (i, 0)),
            const((ATT_WIDTH, d)),
            pl.BlockSpec((HY_WIDTH, d), lambda i: (a0 // HY_WIDTH, 0)),
            pl.BlockSpec((POOL_WIDTH, d), lambda i: (a1 // POOL_WIDTH, 0)),
            _mod_spec(2, row_of_token, tm),
            const((1, d)),
            _mod_spec(4, row_of_token, tm),
            _mod_spec(3, row_of_token, tm),
        ],
        out_specs=[pl.BlockSpec((tm, d), lambda i: (i, 0))] * 2,
        out_shape=[jax.ShapeDtypeStruct((t, d), F32), jax.ShapeDtypeStruct((t, d), BF16)],
        compiler_params=_cparams("parallel"),
        name="out_proj",
    )(att, hy, po, x, w_out, w_out, w_out, mod, norm_g.reshape(1, d), mod, mod)


def _ffn_kernel(h_ref, x_ref, wg_ref, wu_ref, wd_ref, g2_ref, o_ref):
    j = pl.program_id(1)

    @pl.when(j == 0)
    def _():
        o_ref[...] = jnp.zeros_like(o_ref)

    h = h_ref[...]
    gate = _dot(h, wg_ref[...])
    up = _dot(h, wu_ref[...])
    act = (gate * (1.0 / (1.0 + jnp.exp(-gate))) * up).astype(BF16)
    o_ref[...] += _dot(act, wd_ref[...])

    @pl.when(j == pl.num_programs(1) - 1)
    def _():
        o_ref[...] = x_ref[...] + g2_ref[...] * o_ref[...]


def ffn(h2, x1, w_gate, w_up, w_down, mod, row_of_token, tm=1024, th=512):
    t, d = x1.shape
    hidden = w_gate.shape[1]
    tm = min(tm, t)
    return pl.pallas_call(
        _ffn_kernel,
        grid=(t // tm, hidden // th),
        in_specs=[
            pl.BlockSpec((tm, d), lambda i, j: (i, 0)),
            pl.BlockSpec((tm, d), lambda i, j: (i, 0), pipeline_mode=pl.Buffered(1)),
            pl.BlockSpec((d, th), lambda i, j: (0, j)),
            pl.BlockSpec((d, th), lambda i, j: (0, j)),
            pl.BlockSpec((th, d), lambda i, j: (j, 0)),
            _mod_spec(5, row_of_token, tm),
        ],
        out_specs=pl.BlockSpec((tm, d), lambda i, j: (i, 0)),
        out_shape=jax.ShapeDtypeStruct((t, d), F32),
        compiler_params=_cparams("parallel", "arbitrary"),
        name="ffn",
    )(h2, x1, w_gate, w_up, w_down, mod)


def trunk_layer(x, batch, seq_len, mod, row_of_token, rope_tabs, cache, layer, lam_init, tabs, p, emit_state):
    proj = in_proj(x, p["norm1"], mod, row_of_token, p["w_in"], F32 if emit_state else BF16)
    prep = qk_prep(proj, p["q_norm"], p["k_norm"], rope_tabs, seq_len, emit_state)
    cache_k, cache_v = cache if cache is not None else (None, None)
    att = diff_attn(prep[0], prep[1], proj, cache_k, cache_v, layer, p["w_lambda"], p["subln"], lam_init,
                    batch, seq_len)
    hy = hyena_mixer(proj, batch, seq_len, tabs, p["w_sc"], p["b_sc"], p["hy_w1"], p["hy_b1"], p["hy_fr1"],
                     p["hy_w2"], p["hy_b2"], p["hy_fr2"], p["hy_w3"], p["hy_d"])
    po = pool_mixer(proj, p["w_pool"], p["pool_scale"], batch, seq_len)
    x1, h2 = out_proj(att, hy, po, x, p["w_out"], p["norm2"], mod, row_of_token)
    y = ffn(h2, x1, p["w_gate"], p["w_up"], p["w_down"], mod, row_of_token)
    if emit_state:
        return y, prep[2], proj[:, 2 * ATT_WIDTH:3 * ATT_WIDTH]
    return y, None, None


def kernel(x_prompt, x_sample, cache_k, cache_v, c, c_ctx, w_ada, b_ada, norm1, norm2, w_in, q_norm, k_norm, w_lambda, subln, w_sc, b_sc, hy_w1, hy_b1, hy_fr1, hy_w2, hy_b2, hy_fr2, hy_w3, hy_d, w_pool, pool_scale, w_out, w_gate, w_up, w_down):
    batch, seq, d = x_prompt.shape
    dec_batch, dec_seq, _ = x_sample.shape
    depth = w_ada.shape[0]
    past = cache_k.shape[2]

    cond = jnp.concatenate([c_ctx[None, :], c, jnp.zeros((MOD_ROWS - 1 - dec_batch, d), F32)], axis=0)
    mod_all = ada_mod(cond, w_ada, b_ada)
    rope_tabs = rope_tables(dec_seq)
    tabs_ctx, tabs_lat = dft_tables(seq), dft_tables(dec_seq)
    ck = cache_k.reshape(dec_batch, depth, past, ATT_WIDTH)
    cv = cache_v.reshape(dec_batch, depth, past, ATT_WIDTH)

    y_p = x_prompt.reshape(batch * seq, d)
    y_s = x_sample.reshape(dec_batch * dec_seq, d)
    ks, vs = [], []
    for l in range(depth):
        p = dict(norm1=norm1[l], norm2=norm2[l], w_in=w_in[l].astype(BF16), q_norm=q_norm[l], k_norm=k_norm[l],
                 w_lambda=w_lambda[l], subln=subln[l], w_sc=w_sc[l], b_sc=b_sc[l], hy_w1=hy_w1[l],
                 hy_b1=hy_b1[l], hy_fr1=hy_fr1[l], hy_w2=hy_w2[l], hy_b2=hy_b2[l], hy_fr2=hy_fr2[l],
                 hy_w3=hy_w3[l], hy_d=hy_d[l], w_pool=w_pool[l], pool_scale=pool_scale[l],
                 w_out=w_out[l].astype(BF16), w_gate=w_gate[l].astype(BF16), w_up=w_up[l].astype(BF16),
                 w_down=w_down[l].astype(BF16))
        lam_init = 0.8 - 0.6 * math.exp(-0.3 * l)
        mod = mod_all[l].reshape(MOD_ROWS * 6, 1, d)
        y_p, k_l, v_l = trunk_layer(y_p, batch, seq, mod, lambda tok: 0, None, None, l, lam_init,
                                    tabs_ctx, p, True)
        ks.append(k_l.reshape(batch, seq, N_HEADS, 2, ATT_HD))
        vs.append(v_l.reshape(batch, seq, N_HEADS, ATT_VD))
        y_s, _, _ = trunk_layer(y_s, dec_batch, dec_seq, mod, lambda tok: 1 + tok // dec_seq, rope_tabs,
                                (ck, cv), l, lam_init, tabs_lat, p, False)
    return (y_p.reshape(batch, seq, d), y_s.reshape(dec_batch, dec_seq, d),
            jnp.stack(ks, axis=1), jnp.stack(vs, axis=1))
```

```python
import functools
import math

import jax
import jax.numpy as jnp
from jax import lax
from jax.experimental import pallas as pl
from jax.experimental.pallas import tpu as pltpu

F32 = jnp.float32
BF16 = jnp.bfloat16

D_MODEL = 2048
DEPTH = 2
GRID_W = 64
ATT_WIDTH = 1024
HY_WIDTH = 512
POOL_WIDTH = 512
ATT_HD = 64
ATT_VD = 128
N_HEADS = 8
ROPE_THETA = 10000.0
HY_ORDER = 2
HY_BANDS = 16
HY_EMB = 1 + 2 * HY_BANDS
HY_HIDDEN = 64
HY_DECAY_TARGET = 1e-2
HY_FAST_DECAY = 0.3
HY_SLOW_DECAY = 1.5
HY_DECAY_SHIFT = 0.05
HY_MIN_DECAY = math.log(HY_DECAY_TARGET) / HY_SLOW_DECAY
HY_MAX_DECAY = math.log(HY_DECAY_TARGET) / HY_FAST_DECAY
POOL_WINDOWS = (2, 4, 8, 16)
POOL_GW = 128
POOL_HALO = 16
FFN_HIDDEN = 5632
IN_COLS = 5120
EPS = 1e-6
Q_SCALE = ATT_HD ** -0.5 * math.log2(math.e)
MOD_ROWS = 16
EMB_PAD = 128
LANES = 128
VMEM_LIMIT = 56 * 1024 * 1024


def _cparams(*sem):
    return pltpu.CompilerParams(dimension_semantics=sem, vmem_limit_bytes=VMEM_LIMIT)


def _split(x):
    hi = x.astype(BF16)
    lo = (x - hi.astype(F32)).astype(BF16)
    return hi, lo


def _dot(a, b):
    return jnp.dot(a, b, preferred_element_type=F32)


def _dot3(a, b):
    ah, al = _split(a)
    bh, bl = _split(b)
    return _dot(ah, bh) + _dot(al, bh) + _dot(ah, bl)


def _ada_kernel(c_ref, w_ref, b_ref, o_ref):
    c = c_ref[...]
    a = c * (1.0 / (1.0 + jnp.exp(-c)))
    o_ref[...] = _dot3(a, w_ref[...]) + b_ref[...]


def ada_mod(cond, w_ada, b_ada, tn=1024):
    depth, d, n = w_ada.shape
    rows = cond.shape[0]
    return pl.pallas_call(
        _ada_kernel,
        grid=(depth, n // tn),
        in_specs=[
            pl.BlockSpec((rows, d), lambda l, j: (0, 0)),
            pl.BlockSpec((None, d, tn), lambda l, j: (l, 0, j)),
            pl.BlockSpec((None, 1, tn), lambda l, j: (l, 0, j)),
        ],
        out_specs=pl.BlockSpec((None, rows, tn), lambda l, j: (l, 0, j)),
        out_shape=jax.ShapeDtypeStruct((depth, rows, n), F32),
        compiler_params=_cparams("parallel", "parallel"),
        name="ada_mod",
    )(cond, w_ada, b_ada.reshape(depth, 1, n))


def _mod_spec(part, row_of_token, tm):
    return pl.BlockSpec((None, 1, D_MODEL), lambda i, *_: (row_of_token(i * tm) * 6 + part, 0, 0))


def _inproj_kernel(x_ref, g_ref, sc_ref, sh_ref, w_ref, o_ref, h_ref):
    @pl.when(pl.program_id(1) == 0)
    def _():
        x = x_ref[...]
        y = x * lax.rsqrt(jnp.mean(x * x, axis=-1, keepdims=True) + EPS) * g_ref[...]
        h_ref[...] = (y * (1.0 + sc_ref[...]) + sh_ref[...]).astype(BF16)

    o_ref[...] = _dot(h_ref[...], w_ref[...]).astype(o_ref.dtype)


def in_proj(x, norm_g, mod, row_of_token, w_in, out_dtype, tm=1024, tn=1024):
    t, d = x.shape
    n = w_in.shape[1]
    tm = min(tm, t)
    return pl.pallas_call(
        _inproj_kernel,
        grid=(t // tm, n // tn),
        in_specs=[
            pl.BlockSpec((tm, d), lambda i, j: (i, 0)),
            pl.BlockSpec((1, d), lambda i, j: (0, 0)),
            _mod_spec(1, row_of_token, tm),
            _mod_spec(0, row_of_token, tm),
            pl.BlockSpec((d, tn), lambda i, j: (0, j)),
        ],
        out_specs=pl.BlockSpec((tm, tn), lambda i, j: (i, j)),
        out_shape=jax.ShapeDtypeStruct((t, n), out_dtype),
        scratch_shapes=[pltpu.VMEM((tm, d), BF16)],
        compiler_params=_cparams("parallel", "arbitrary"),
        name="in_proj",
    )(x, norm_g.reshape(1, d), mod, mod, w_in)


def _group_rms(y, lane):
    ss = y * y
    lo = lane < ATT_HD
    s_lo = jnp.sum(jnp.where(lo, ss, 0.0), axis=-1, keepdims=True)
    s_hi = jnp.sum(jnp.where(lo, 0.0, ss), axis=-1, keepdims=True)
    inv = jnp.where(lo, lax.rsqrt(s_lo * (1.0 / ATT_HD) + EPS), lax.rsqrt(s_hi * (1.0 / ATT_HD) + EPS))
    return y * inv


def _rope(y, cos, sin, lane):
    swap = jnp.where((lane & (ATT_HD // 2)) == 0,
                     pltpu.roll(y, LANES - ATT_HD // 2, 1), pltpu.roll(y, ATT_HD // 2, 1))
    return y * cos + swap * sin


def _qkprep_kernel(*refs, rope, emit_state):
    if rope:
        q_ref, k_ref, gq_ref, gk_ref, cos_ref, sin_ref = refs[:6]
        outs = refs[6:]
    else:
        q_ref, k_ref, gq_ref, gk_ref = refs[:4]
        outs = refs[4:]
    qo_ref, ko_ref = outs[0], outs[1]
    tm = q_ref.shape[0]
    lane = lax.broadcasted_iota(jnp.int32, (tm, LANES), 1)
    for h in range(N_HEADS):
        cs = slice(h * LANES, (h + 1) * LANES)
        q = _group_rms(q_ref[:, cs].astype(F32), lane) * gq_ref[...]
        k = _group_rms(k_ref[:, cs].astype(F32), lane) * gk_ref[...]
        if emit_state:
            outs[2][:, cs] = k
        if rope:
            q = _rope(q, cos_ref[...], sin_ref[...], lane)
            k = _rope(k, cos_ref[...], sin_ref[...], lane)
        qo_ref[:, cs] = (q * Q_SCALE).astype(BF16)
        ko_ref[:, cs] = k.astype(BF16)


def qk_prep(proj, q_gain, k_gain, rope_tabs, seq_len, emit_state, tm=512):
    t = proj.shape[0]
    tm = min(tm, seq_len)
    gq = jnp.tile(q_gain.reshape(1, ATT_HD), (1, 2))
    gk = jnp.tile(k_gain.reshape(1, ATT_HD), (1, 2))
    in_specs = [
        pl.BlockSpec((tm, ATT_WIDTH), lambda i: (i, 0)),
        pl.BlockSpec((tm, ATT_WIDTH), lambda i: (i, 1)),
        pl.BlockSpec((1, LANES), lambda i: (0, 0)),
        pl.BlockSpec((1, LANES), lambda i: (0, 0)),
    ]
    args = [proj, proj, gq, gk]
    if rope_tabs is not None:
        nblk = seq_len // tm
        in_specs += [pl.BlockSpec((tm, LANES), lambda i: (i % nblk, 0))] * 2
        args += list(rope_tabs)
    out_shape = [jax.ShapeDtypeStruct((t, ATT_WIDTH), BF16)] * 2
    out_specs = [pl.BlockSpec((tm, ATT_WIDTH), lambda i: (i, 0))] * 2
    if emit_state:
        out_shape.append(jax.ShapeDtypeStruct((t, ATT_WIDTH), F32))
        out_specs.append(pl.BlockSpec((tm, ATT_WIDTH), lambda i: (i, 0)))
    return pl.pallas_call(
        functools.partial(_qkprep_kernel, rope=rope_tabs is not None, emit_state=emit_state),
        grid=(t // tm,),
        in_specs=in_specs,
        out_specs=out_specs,
        out_shape=out_shape,
        compiler_params=_cparams("parallel"),
        name="qk_prep",
    )(*args)


def rope_tables(n_tokens):
    pos = jnp.arange(n_tokens)
    rows = (pos // GRID_W).astype(F32)
    cols = (pos % GRID_W).astype(F32)
    quarter = ATT_HD // 4
    inv = ROPE_THETA ** (-jnp.arange(quarter, dtype=F32) / quarter)
    ang = jnp.concatenate([rows[:, None] * inv, cols[:, None] * inv], -1)
    cos, sin = jnp.cos(ang), jnp.sin(ang)
    return jnp.tile(cos, (1, 4)), jnp.concatenate([-sin, sin, -sin, sin], -1)


def _attn_kernel(*refs, n_chunks, tk, has_ctx, lam_init):
    if has_ctx:
        q_ref, k_ref, v_ref, ck_ref, cv_ref, lw_ref, sub_ref, o_ref, s_ref, ka_ref, va_ref = refs
    else:
        q_ref, k_ref, v_ref, lw_ref, sub_ref, o_ref, s_ref, ka_ref, va_ref = refs
    tq = q_ref.shape[0]
    seq = k_ref.shape[0]

    @pl.when(pl.program_id(2) == 0)
    def _():
        ka_ref[0:seq, :] = k_ref[...]
        va_ref[0:seq, :] = v_ref[...].astype(BF16)
        if has_ctx:
            ka_ref[seq:, :] = ck_ref[...].astype(BF16)
            va_ref[seq:, :] = cv_ref[...].astype(BF16)

    q = q_ref[...]
    lane = lax.broadcasted_iota(jnp.int32, (tq, LANES), 1)
    zero = jnp.zeros_like(q)
    qz = jnp.concatenate([jnp.where(lane < ATT_HD, q, zero), jnp.where(lane < ATT_HD, zero, q)], axis=0)

    def scores(kc):
        return lax.dot_general(qz, kc, (((1,), (1,)), ((), ())), preferred_element_type=F32)

    def fold_max(m, s):
        for t in range(s.shape[1] // LANES):
            m = jnp.maximum(m, s[:, t * LANES:(t + 1) * LANES])
        return m

    def weighted(s, vc):
        p = jnp.exp2(s - jnp.concatenate([mb] * (s.shape[1] // LANES), axis=1)).astype(BF16)
        return _dot(p, jnp.concatenate([vc, jnp.ones_like(vc)], axis=1))

    def pass1(c, mrun):
        off = pl.multiple_of(c * tk, LANES)
        s = scores(ka_ref[pl.ds(off, tk), :])
        s_ref[:, pl.ds(off, tk)] = s
        return fold_max(mrun, s)

    mrun = lax.fori_loop(0, n_chunks, pass1, jnp.full((2 * tq, LANES), -jnp.inf, F32))
    mb = jnp.broadcast_to(jnp.max(mrun, axis=-1, keepdims=True), (2 * tq, LANES))

    def pass2(c, acc):
        off = pl.multiple_of(c * tk, LANES)
        return acc + weighted(s_ref[:, pl.ds(off, tk)], va_ref[pl.ds(off, tk), :])

    acc = lax.fori_loop(0, n_chunks, pass2, jnp.zeros((2 * tq, 2 * ATT_VD), F32))
    acc, l = acc[:, :ATT_VD], acc[:, ATT_VD:]

    lw = lw_ref[...]
    lam = (jnp.exp(jnp.sum(lw[0:1] * lw[1:2], axis=-1, keepdims=True))
           - jnp.exp(jnp.sum(lw[2:3] * lw[3:4], axis=-1, keepdims=True)) + lam_init)
    o = acc[:tq] / l[:tq] - lam * (acc[tq:] / l[tq:])
    o = o * lax.rsqrt(jnp.mean(o * o, axis=-1, keepdims=True) + EPS) * sub_ref[...]
    o_ref[...] = (o * (1.0 - lam_init)).astype(o_ref.dtype)


def diff_attn(qn, kn, proj, cache_k, cache_v, layer, w_lambda, subln, lam_init, batch, seq_len,
              tq=512, max_tk=2304):
    t = qn.shape[0]
    tq = min(tq, seq_len)
    nq = seq_len // tq
    has_ctx = cache_k is not None
    total = seq_len + (cache_k.shape[2] if has_ctx else 0)
    n_chunks = -(-total // max_tk)
    tk = total // n_chunks
    assert tk * n_chunks == total and tk % LANES == 0
    v_col0 = 2 * ATT_WIDTH // LANES
    in_specs = [
        pl.BlockSpec((tq, LANES), lambda b, h, i: (b * nq + i, h)),
        pl.BlockSpec((seq_len, LANES), lambda b, h, i: (b, h)),
        pl.BlockSpec((seq_len, LANES), lambda b, h, i: (b, v_col0 + h)),
    ]
    args = [qn, kn, proj]
    if has_ctx:
        past = cache_k.shape[2]
        in_specs += [pl.BlockSpec((None, None, past, LANES), lambda b, h, i: (b, layer, 0, h))] * 2
        args += [cache_k, cache_v]
    in_specs += [pl.BlockSpec((4, ATT_HD), lambda b, h, i: (0, 0)),
                 pl.BlockSpec((1, ATT_VD), lambda b, h, i: (0, 0))]
    args += [w_lambda, subln.reshape(1, ATT_VD)]
    return pl.pallas_call(
        functools.partial(_attn_kernel, n_chunks=n_chunks, tk=tk, has_ctx=has_ctx, lam_init=lam_init),
        grid=(batch, N_HEADS, nq),
        in_specs=in_specs,
        out_specs=pl.BlockSpec((tq, LANES), lambda b, h, i: (b * nq + i, h)),
        out_shape=jax.ShapeDtypeStruct((t, ATT_WIDTH), BF16),
        scratch_shapes=[pltpu.VMEM((2 * tq, total), F32),
                        pltpu.VMEM((total, LANES), BF16), pltpu.VMEM((total, LANES), BF16)],
        compiler_params=_cparams("parallel", "parallel", "arbitrary"),
        name="diff_attn",
    )(*args)


def _shortconv_kernel(u_ref, prev_ref, next_ref, w_ref, b_ref, v_ref, x1_ref, x2_ref, *, tiles_per_seq):
    i = pl.program_id(0)
    tl = u_ref.shape[0]
    u = u_ref[...].astype(F32)
    first = (i % tiles_per_seq) == 0
    last = (i % tiles_per_seq) == tiles_per_seq - 1
    sub = prev_ref.shape[0]
    prev_row = jnp.where(first, 0.0, prev_ref[sub - 1:sub, :].astype(F32))
    next_row = jnp.where(last, 0.0, next_ref[0:1, :].astype(F32))
    row = lax.broadcasted_iota(jnp.int32, u.shape, 0)
    below = jnp.where(row == 0, prev_row, pltpu.roll(u, 1, 0))
    above = jnp.where(row == tl - 1, next_row, pltpu.roll(u, tl - 1, 0))
    w = w_ref[...]
    y = below * w[0:1] + u * w[1:2] + above * w[2:3] + b_ref[...]
    v_ref[...] = y[:, :HY_WIDTH]
    x1_ref[...] = y[:, HY_WIDTH:2 * HY_WIDTH]
    x2_ref[...] = y[:, 2 * HY_WIDTH:]


def short_conv(proj, w_sc, b_sc, seq_len, tl=256):
    t = proj.shape[0]
    width = 3 * HY_WIDTH
    cb = 3 * ATT_WIDTH // width
    tl = min(tl, seq_len)
    sub = 16
    nsub = tl // sub
    last_sub = t // sub - 1
    part = jax.ShapeDtypeStruct((t, HY_WIDTH), F32)
    return pl.pallas_call(
        functools.partial(_shortconv_kernel, tiles_per_seq=seq_len // tl),
        grid=(t // tl,),
        in_specs=[
            pl.BlockSpec((tl, width), lambda i: (i, cb)),
            pl.BlockSpec((sub, width), lambda i: (jnp.maximum(i * nsub - 1, 0), cb)),
            pl.BlockSpec((sub, width), lambda i: (jnp.minimum((i + 1) * nsub, last_sub), cb)),
            pl.BlockSpec((3, width), lambda i: (0, 0)),
            pl.BlockSpec((1, width), lambda i: (0, 0)),
        ],
        out_specs=[pl.BlockSpec((tl, HY_WIDTH), lambda i: (i, 0))] * 3,
        out_shape=[part] * 3,
        compiler_params=_cparams("parallel"),
        name="short_conv",
    )(proj, proj, proj, w_sc, b_sc.reshape(1, width))


def _filter_kernel(z_ref, w1_ref, b1_ref, f1_ref, w2_ref, b2_ref, f2_ref, w3f_ref, w3b_ref, dl_ref,
                   o_ref, h_ref, *, seq_len):
    @pl.when(pl.program_id(0) == 0)
    def _():
        h = jnp.sin(f1_ref[...] * (_dot3(z_ref[...], w1_ref[...]) + b1_ref[...]))
        h_ref[...] = jnp.sin(f2_ref[...] * (_dot3(h, w2_ref[...]) + b2_ref[...]))

    n = 2 * seq_len
    t = z_ref[:, 0:1]
    window = jnp.exp(-t * dl_ref[...]) + HY_DECAY_SHIFT
    fwd = _dot3(h_ref[0:seq_len, :], w3f_ref[...])
    bwd = _dot3(h_ref[seq_len:n, :], w3b_ref[...])
    row = lax.broadcasted_iota(jnp.int32, (seq_len, 1), 0)
    bwd = jnp.where(row == 0, 0.0, bwd)
    fwd = fwd * window[0:seq_len]
    bwd = bwd * window[seq_len:n]
    norm = jnp.sum(jnp.abs(fwd), axis=0, keepdims=True) + jnp.sum(jnp.abs(bwd), axis=0, keepdims=True)
    o_ref[0:seq_len, :] = fwd / norm
    o_ref[seq_len:n, :] = bwd / norm


def hyena_embedding(seq_len):
    pos = jnp.concatenate([jnp.arange(seq_len), seq_len - jnp.arange(seq_len)]) % seq_len
    t = jnp.linspace(0.0, 1.0, seq_len, dtype=F32)[pos][:, None]
    wpos = (2.0 * math.pi * jnp.arange(seq_len, dtype=F32) / seq_len)[pos][:, None]
    bands = jnp.linspace(1e-4, HY_BANDS - 1, HY_BANDS, dtype=F32)[None, :]
    z = jnp.concatenate([t, jnp.cos(bands * wpos), jnp.sin(bands * wpos)], -1)
    return jnp.pad(z, ((0, 0), (0, EMB_PAD - HY_EMB)))


def hyena_filter(seq_len, w1, b1, fr1, w2, b2, fr2, w3, tn=128):
    n = 2 * seq_len
    z = hyena_embedding(seq_len)
    w1p = jnp.pad(w1, ((0, EMB_PAD - HY_EMB), (0, 0)))
    deltas = jnp.abs(jnp.linspace(HY_MIN_DECAY, HY_MAX_DECAY, HY_WIDTH, dtype=F32)).reshape(1, HY_WIDTH)
    cpo = HY_WIDTH // tn
    nblk = HY_ORDER * cpo
    row = lambda a: a.reshape(1, -1)
    full = lambda shape: pl.BlockSpec(shape, lambda j: (0,) * len(shape))
    return pl.pallas_call(
        functools.partial(_filter_kernel, seq_len=seq_len),
        grid=(nblk,),
        in_specs=[
            full((n, EMB_PAD)), full((EMB_PAD, HY_HIDDEN)), full((1, HY_HIDDEN)), full((1, HY_HIDDEN)),
            full((HY_HIDDEN, HY_HIDDEN)), full((1, HY_HIDDEN)), full((1, HY_HIDDEN)),
            pl.BlockSpec((HY_HIDDEN, tn), lambda j: (0, j)),
            pl.BlockSpec((HY_HIDDEN, tn), lambda j: (0, nblk + j)),
            pl.BlockSpec((1, tn), lambda j: (0, j % cpo)),
        ],
        out_specs=pl.BlockSpec((None, n, tn), lambda j: (j // cpo, 0, j % cpo)),
        out_shape=jax.ShapeDtypeStruct((HY_ORDER, n, HY_WIDTH), F32),
        scratch_shapes=[pltpu.VMEM((n, HY_HIDDEN), F32)],
        compiler_params=_cparams("arbitrary"),
        name="hyena_filter",
    )(z, w1p, row(b1), row(fr1), w2, row(b2), row(fr2), w3, w3, deltas)


SUB = 8
STAGE_B_MAX_UNROLL = 11
STAGE_AC_UNROLL = 2


def _chain_unroll(trips):
    return max(u for u in range(1, STAGE_B_MAX_UNROLL + 1) if trips % u == 0)


def dft_split(seq_len):
    return (64, 128) if seq_len >= 4096 else (16, 2 * seq_len // 16)


def _angles(a, b, period):
    return (2.0 * math.pi / period) * ((a * b) % period).astype(F32)


def dft_tables(seq_len):
    n1, n2 = dft_split(seq_len)
    n = n1 * n2
    kept = n1 // 2 + 1
    k1 = jnp.arange(n1)
    th = _angles(k1[:kept, None], k1[None, :], n1)
    c, s = jnp.cos(th), jnp.sin(th)
    fa = jnp.stack([c, -s], axis=1).reshape(2 * kept, n1)
    herm = jnp.where((k1[:kept] == 0) | (k1[:kept] == n1 // 2), 1.0, 2.0)[None, :, None]
    fc = jnp.stack([c.T, -s.T], axis=2)[: n1 // 2] * herm / n
    fc = fc.reshape(n1 // 2, 2 * kept)
    eye = jnp.eye(SUB, dtype=F32)
    k1 = k1[:kept]
    kk = k1[:, None, None] + n1 * jnp.arange(n2)[None, :, None]
    th = _angles(kk, jnp.arange(n2)[None, None, :], n)
    c, s = jnp.cos(th), jnp.sin(th)
    gf = jnp.concatenate([jnp.concatenate([c, s], 2), jnp.concatenate([-s, c], 2)], 1)
    ct, st = jnp.swapaxes(c, 1, 2), jnp.swapaxes(s, 1, 2)
    gi = jnp.concatenate([jnp.concatenate([ct, -st], 2), jnp.concatenate([st, ct], 2)], 1)
    return dict(n1=n1, n2=n2, kept=kept,
                fk_full=_split(jnp.kron(fa, eye)),
                fk=jnp.kron(fa[:, : n1 // 2], eye).astype(BF16),
                fki=jnp.kron(fc, eye).astype(BF16),
                gf=_split(gf), gi=gi.astype(BF16))


def _resident(shape):
    return pl.BlockSpec(shape, lambda *_: (0,) * len(shape), pipeline_mode=pl.Buffered(1))


def _spectrum_kernel(k_ref, fkh_ref, fkl_ref, gh_ref, gl_ref, o_ref, a_ref):
    rows, groups, _, tc = k_ref.shape
    n1 = gh_ref.shape[0]
    unroll = _chain_unroll(n1)

    def stage_a(g, carry):
        xh, xl = _split(k_ref[:, g].reshape(rows * SUB, tc))
        r = _dot(fkh_ref[...], xh) + _dot(fkh_ref[...], xl) + _dot(fkl_ref[...], xh)
        a_ref[:, :, g] = r.reshape(n1, 2, SUB, tc)
        return carry

    lax.fori_loop(0, groups, stage_a, 0)

    def stage_b(kk, carry):
        for u in range(unroll):
            k = kk * unroll + u
            ah, al = _split(a_ref[k].reshape(2 * groups * SUB, tc))
            o_ref[k] = _dot(gh_ref[k], ah) + _dot(gh_ref[k], al) + _dot(gl_ref[k], ah)
        return carry

    lax.fori_loop(0, n1 // unroll, stage_b, 0)


def filter_spectrum(kc, tabs, tc):
    n1, n2, kept = tabs["n1"], tabs["n2"], tabs["kept"]
    orders, n, c = kc.shape
    groups = n2 // SUB
    fkh, fkl = tabs["fk_full"]
    gh, gl = tabs["gf"]
    return pl.pallas_call(
        _spectrum_kernel,
        grid=(orders, c // tc),
        in_specs=[pl.BlockSpec((None, n1, groups, SUB, tc), lambda o, j: (o, 0, 0, 0, j)),
                  _resident(fkh.shape), _resident(fkl.shape), _resident(gh.shape), _resident(gl.shape)],
        out_specs=pl.BlockSpec((None, kept, 2 * n2, tc), lambda o, j: (o, 0, 0, j)),
        out_shape=jax.ShapeDtypeStruct((orders, kept, 2 * n2, c), F32),
        scratch_shapes=[pltpu.VMEM((kept, 2, groups, SUB, tc), F32)],
        compiler_params=_cparams("parallel", "arbitrary"),
        name="hyena_filter_spectrum",
    )(kc.reshape(orders, n1, groups, SUB, c), fkh, fkl, gh, gl)


def _hyena_conv_kernel(z_ref, x_ref, d_ref, kf_ref, fk_ref, fki_ref, gf_ref, gi_ref, o_ref, a_ref):
    rows, groups, _, tc = z_ref.shape
    n1 = gf_ref.shape[0]
    unroll = _chain_unroll(n1)
    half = groups * SUB

    def stage_a(gg, carry):
        for u in range(STAGE_AC_UNROLL):
            g = gg * STAGE_AC_UNROLL + u
            xg = z_ref[:, g].reshape(rows * SUB, tc).astype(BF16)
            a_ref[:, :, g] = _dot(fk_ref[...], xg).reshape(n1, 2, SUB, tc)
        return carry

    lax.fori_loop(0, groups // STAGE_AC_UNROLL, stage_a, 0)

    def stage_b(kk, carry):
        for u in range(unroll):
            k = kk * unroll + u
            x = _dot(gf_ref[k], a_ref[k].reshape(2 * half, tc).astype(BF16))
            kf = kf_ref[k]
            xr, xi = x[:half], x[half:]
            kr, ki = kf[:half], kf[half:]
            p = jnp.concatenate([xr * kr - xi * ki, xr * ki + xi * kr], axis=0).astype(BF16)
            a_ref[k] = _dot(gi_ref[k], p).reshape(2, groups, SUB, tc)
        return carry

    lax.fori_loop(0, n1 // unroll, stage_b, 0)

    d = d_ref[...].reshape(1, 1, tc)

    def stage_c(gg, carry):
        for u in range(STAGE_AC_UNROLL):
            g = gg * STAGE_AC_UNROLL + u
            bg = a_ref[:, :, g].reshape(n1 * 2 * SUB, tc).astype(BF16)
            y = _dot(fki_ref[...], bg).reshape(rows, SUB, tc)
            o_ref[:, g] = x_ref[:, g] * (y + z_ref[:, g] * d)
        return carry

    lax.fori_loop(0, groups // STAGE_AC_UNROLL, stage_c, 0)


def hyena_conv(z, x, d, kf, order, tabs, tc):
    n1, n2, kept = tabs["n1"], tabs["n2"], tabs["kept"]
    b, seq_len, c = z.shape
    rows, groups = n1 // 2, n2 // SUB
    view = lambda a: a.reshape(b, rows, groups, SUB, c)
    seq_spec = pl.BlockSpec((None, rows, groups, SUB, tc), lambda j, i: (i, 0, 0, 0, j))
    gf = tabs["gf"][0]
    out = pl.pallas_call(
        _hyena_conv_kernel,
        grid=(c // tc, b),
        in_specs=[seq_spec, seq_spec,
                  pl.BlockSpec((1, tc), lambda j, i: (0, j)),
                  pl.BlockSpec((None, kept, 2 * n2, tc), lambda j, i: (order, 0, 0, j),
                               pipeline_mode=pl.Buffered(1)),
                  _resident(tabs["fk"].shape), _resident(tabs["fki"].shape),
                  _resident(gf.shape), _resident(tabs["gi"].shape)],
        out_specs=seq_spec,
        out_shape=jax.ShapeDtypeStruct((b, rows, groups, SUB, c), F32),
        scratch_shapes=[pltpu.VMEM((kept, 2, groups, SUB, tc), F32)],
        compiler_params=_cparams("parallel", "arbitrary"),
        name="hyena_conv",
    )(view(z), view(x), d.reshape(1, c), kf, tabs["fk"], tabs["fki"], gf, tabs["gi"])
    return out.reshape(b, seq_len, c)


def hyena_mixer(proj, batch, seq_len, tabs, w_sc, b_sc, w1, b1, fr1, w2, b2, fr2, w3, hy_d):
    c = HY_WIDTH
    kc = hyena_filter(seq_len, w1, b1, fr1, w2, b2, fr2, w3)
    tc = 2 * LANES if seq_len >= 4096 else c
    kf = filter_spectrum(kc, tabs, tc)
    v, x1, x2 = short_conv(proj, w_sc, b_sc, seq_len)
    z = v.reshape(batch, seq_len, c)
    for o, x in enumerate((x1, x2)):
        z = hyena_conv(z, x.reshape(batch, seq_len, c), hy_d[o], kf, o, tabs, tc)
    return z.reshape(batch * seq_len, c)


def _pool_kernel(u_ref, w_ref, s_ref, o_ref, pad_ref):
    seq_len = u_ref.shape[0]
    h = POOL_HALO
    zeros = jnp.zeros((h, POOL_WIDTH), F32)
    pad_ref[0:h, :] = zeros
    pad_ref[h + seq_len:h + seq_len + h, :] = zeros
    pad_ref[h:h + seq_len, :] = u_ref[...].astype(F32)
    t = lax.broadcasted_iota(jnp.int32, (seq_len, 1), 0)
    for g, w in enumerate(POOL_WINDOWS):
        cs = slice(g * POOL_GW, (g + 1) * POOL_GW)
        tot = pad_ref[h - w // 2:h - w // 2 + seq_len, cs]
        for d in range(1 - w // 2, w - w // 2):
            tot = tot + pad_ref[h + d:h + d + seq_len, cs]
        cnt = (jnp.minimum(t - w // 2 + w, seq_len) - jnp.maximum(t - w // 2, 0)).astype(F32)
        pooled = tot / cnt - pad_ref[h:h + seq_len, cs]
        o_ref[:, cs] = (_dot(pooled.astype(BF16), w_ref[g]) * s_ref[:, cs]).astype(o_ref.dtype)


def pool_mixer(proj, w_pool, pool_scale, batch, seq_len):
    t = proj.shape[0]
    cb = (IN_COLS - POOL_WIDTH) // POOL_WIDTH
    return pl.pallas_call(
        _pool_kernel,
        grid=(batch,),
        in_specs=[pl.BlockSpec((seq_len, POOL_WIDTH), lambda b: (b, cb)),
                  pl.BlockSpec((len(POOL_WINDOWS), POOL_GW, POOL_GW), lambda b: (0, 0, 0)),
                  pl.BlockSpec((1, POOL_WIDTH), lambda b: (0, 0))],
        out_specs=pl.BlockSpec((seq_len, POOL_WIDTH), lambda b: (b, 0)),
        out_shape=jax.ShapeDtypeStruct((t, POOL_WIDTH), BF16),
        scratch_shapes=[pltpu.VMEM((seq_len + 2 * POOL_HALO, POOL_WIDTH), F32)],
        compiler_params=_cparams("parallel"),
        name="pool_mixer",
    )(proj, w_pool.astype(BF16), pool_scale.reshape(1, POOL_WIDTH))


def _outproj_kernel(att_ref, hy_ref, po_ref, x_ref, wa_ref, wh_ref, wp_ref, g1_ref, g_ref, sc_ref, sh_ref,
                    x1_ref, h2_ref):
    mix = (_dot(att_ref[...], wa_ref[...]) + _dot(hy_ref[...].astype(BF16), wh_ref[...])
           + _dot(po_ref[...], wp_ref[...]))
    x1 = x_ref[...] + g1_ref[...] * mix
    x1_ref[...] = x1
    y = x1 * lax.rsqrt(jnp.mean(x1 * x1, axis=-1, keepdims=True) + EPS) * g_ref[...]
    h2_ref[...] = (y * (1.0 + sc_ref[...]) + sh_ref[...]).astype(BF16)


def out_proj(att, hy, po, x, w_out, norm_g, mod, row_of_token, tm=512):
    t, d = x.shape
    a0, a1 = ATT_WIDTH, ATT_WIDTH + HY_WIDTH
    const = lambda shape: pl.BlockSpec(shape, lambda i: (0, 0))
    return pl.pallas_call(
        _outproj_kernel,
        grid=(t // tm,),
        in_specs=[
            pl.BlockSpec((tm, ATT_WIDTH), lambda i: (i, 0)),
            pl.BlockSpec((tm, HY_WIDTH), lambda i: (i, 0)),
            pl.BlockSpec((tm, POOL_WIDTH), lambda i: (i, 0)),
            pl.BlockSpec((tm, d), lambda i: (i, 0)),
            const((ATT_WIDTH, d)),
            pl.BlockSpec((HY_WIDTH, d), lambda i: (a0 // HY_WIDTH, 0)),
            pl.BlockSpec((POOL_WIDTH, d), lambda i: (a1 // POOL_WIDTH, 0)),
            _mod_spec(2, row_of_token, tm),
            const((1, d)),
            _mod_spec(4, row_of_token, tm),
            _mod_spec(3, row_of_token, tm),
        ],
        out_specs=[pl.BlockSpec((tm, d), lambda i: (i, 0))] * 2,
        out_shape=[jax.ShapeDtypeStruct((t, d), F32), jax.ShapeDtypeStruct((t, d), BF16)],
        compiler_params=_cparams("parallel"),
        name="out_proj",
    )(att, hy, po, x, w_out, w_out, w_out, mod, norm_g.reshape(1, d), mod, mod)


def _ffn_kernel(h_ref, x_ref, wg_ref, wu_ref, wd_ref, g2_ref, o_ref, acc_ref):
    j = pl.program_id(1)

    @pl.when(j == 0)
    def _():
        acc_ref[...] = jnp.zeros_like(acc_ref)

    h = h_ref[...]
    gate = _dot(h, wg_ref[...])
    up = _dot(h, wu_ref[...])
    act = (gate * (1.0 / (1.0 + jnp.exp(-gate))) * up).astype(BF16)
    acc_ref[...] += _dot(act, wd_ref[...])

    @pl.when(j == pl.num_programs(1) - 1)
    def _():
        o_ref[...] = x_ref[...] + g2_ref[...] * acc_ref[...]


def ffn(h2, x1, w_gate, w_up, w_down, mod, row_of_token, tm=512, th=512):
    t, d = x1.shape
    hidden = w_gate.shape[1]
    tm = min(tm, t)
    return pl.pallas_call(
        _ffn_kernel,
        grid=(t // tm, hidden // th),
        in_specs=[
            pl.BlockSpec((tm, d), lambda i, j: (i, 0)),
            pl.BlockSpec((tm, d), lambda i, j: (i, 0)),
            pl.BlockSpec((d, th), lambda i, j: (0, j)),
            pl.BlockSpec((d, th), lambda i, j: (0, j)),
            pl.BlockSpec((th, d), lambda i, j: (j, 0)),
            _mod_spec(5, row_of_token, tm),
        ],
        out_specs=pl.BlockSpec((tm, d), lambda i, j: (i, 0)),
        out_shape=jax.ShapeDtypeStruct((t, d), F32),
        scratch_shapes=[pltpu.VMEM((tm, d), F32)],
        compiler_params=_cparams("parallel", "arbitrary"),
        name="ffn",
    )(h2, x1, w_gate, w_up, w_down, mod)


def trunk_layer(x, batch, seq_len, mod, row_of_token, rope_tabs, cache, layer, lam_init, tabs, p, emit_state):
    proj = in_proj(x, p["norm1"], mod, row_of_token, p["w_in"], F32 if emit_state else BF16)
    prep = qk_prep(proj, p["q_norm"], p["k_norm"], rope_tabs, seq_len, emit_state)
    cache_k, cache_v = cache if cache is not None else (None, None)
    att = diff_attn(prep[0], prep[1], proj, cache_k, cache_v, layer, p["w_lambda"], p["subln"], lam_init,
                    batch, seq_len)
    hy = hyena_mixer(proj, batch, seq_len, tabs, p["w_sc"], p["b_sc"], p["hy_w1"], p["hy_b1"], p["hy_fr1"],
                     p["hy_w2"], p["hy_b2"], p["hy_fr2"], p["hy_w3"], p["hy_d"])
    po = pool_mixer(proj, p["w_pool"], p["pool_scale"], batch, seq_len)
    x1, h2 = out_proj(att, hy, po, x, p["w_out"], p["norm2"], mod, row_of_token)
    y = ffn(h2, x1, p["w_gate"], p["w_up"], p["w_down"], mod, row_of_token)
    if emit_state:
        return y, prep[2], proj[:, 2 * ATT_WIDTH:3 * ATT_WIDTH]
    return y, None, None


def kernel(x_prompt, x_sample, cache_k, cache_v, c, c_ctx, w_ada, b_ada, norm1, norm2, w_in, q_norm, k_norm, w_lambda, subln, w_sc, b_sc, hy_w1, hy_b1, hy_fr1, hy_w2, hy_b2, hy_fr2, hy_w3, hy_d, w_pool, pool_scale, w_out, w_gate, w_up, w_down):
    batch, seq, d = x_prompt.shape
    dec_batch, dec_seq, _ = x_sample.shape
    depth = w_ada.shape[0]
    past = cache_k.shape[2]

    cond = jnp.concatenate([c_ctx[None, :], c, jnp.zeros((MOD_ROWS - 1 - dec_batch, d), F32)], axis=0)
    mod_all = ada_mod(cond, w_ada, b_ada)
    rope_tabs = rope_tables(dec_seq)
    tabs_ctx, tabs_lat = dft_tables(seq), dft_tables(dec_seq)
    ck = cache_k.reshape(dec_batch, depth, past, ATT_WIDTH)
    cv = cache_v.reshape(dec_batch, depth, past, ATT_WIDTH)

    y_p = x_prompt.reshape(batch * seq, d)
    y_s = x_sample.reshape(dec_batch * dec_seq, d)
    ks, vs = [], []
    for l in range(depth):
        p = dict(norm1=norm1[l], norm2=norm2[l], w_in=w_in[l].astype(BF16), q_norm=q_norm[l], k_norm=k_norm[l],
                 w_lambda=w_lambda[l], subln=subln[l], w_sc=w_sc[l], b_sc=b_sc[l], hy_w1=hy_w1[l],
                 hy_b1=hy_b1[l], hy_fr1=hy_fr1[l], hy_w2=hy_w2[l], hy_b2=hy_b2[l], hy_fr2=hy_fr2[l],
                 hy_w3=hy_w3[l], hy_d=hy_d[l], w_pool=w_pool[l], pool_scale=pool_scale[l],
                 w_out=w_out[l].astype(BF16), w_gate=w_gate[l].astype(BF16), w_up=w_up[l].astype(BF16),
                 w_down=w_down[l].astype(BF16))
        lam_init = 0.8 - 0.6 * math.exp(-0.3 * l)
        mod = mod_all[l].reshape(MOD_ROWS * 6, 1, d)
        y_p, k_l, v_l = trunk_layer(y_p, batch, seq, mod, lambda tok: 0, None, None, l, lam_init,
                                    tabs_ctx, p, True)
        ks.append(k_l.reshape(batch, seq, N_HEADS, 2, ATT_HD))
        vs.append(v_l.reshape(batch, seq, N_HEADS, ATT_VD))
        y_s, _, _ = trunk_layer(y_s, dec_batch, dec_seq, mod, lambda tok: 1 + tok // dec_seq, rope_tabs,
                                (ck, cv), l, lam_init, tabs_lat, p, False)
    return (y_p.reshape(batch, seq, d), y_s.reshape(dec_batch, dec_seq, d),
            jnp.stack(ks, axis=1), jnp.stack(vs, axis=1))
```

```python
import functools
import math

import jax
import jax.numpy as jnp
from jax import lax
from jax.experimental import pallas as pl
from jax.experimental.pallas import tpu as pltpu

F32 = jnp.float32
BF16 = jnp.bfloat16

D_MODEL = 2048
DEPTH = 2
GRID_W = 64
ATT_WIDTH = 1024
HY_WIDTH = 512
POOL_WIDTH = 512
ATT_HD = 64
ATT_VD = 128
N_HEADS = 8
ROPE_THETA = 10000.0
HY_ORDER = 2
HY_BANDS = 16
HY_EMB = 1 + 2 * HY_BANDS
HY_HIDDEN = 64
HY_DECAY_TARGET = 1e-2
HY_FAST_DECAY = 0.3
HY_SLOW_DECAY = 1.5
HY_DECAY_SHIFT = 0.05
HY_MIN_DECAY = math.log(HY_DECAY_TARGET) / HY_SLOW_DECAY
HY_MAX_DECAY = math.log(HY_DECAY_TARGET) / HY_FAST_DECAY
POOL_WINDOWS = (2, 4, 8, 16)
POOL_GW = 128
POOL_HALO = 16
FFN_HIDDEN = 5632
IN_COLS = 5120
EPS = 1e-6
Q_SCALE = ATT_HD ** -0.5 * math.log2(math.e)
MOD_ROWS = 16
EMB_PAD = 128
LANES = 128
MXU_WIDTH = 256
VMEM_LIMIT = 56 * 1024 * 1024


def _cparams(*sem):
    return pltpu.CompilerParams(dimension_semantics=sem, vmem_limit_bytes=VMEM_LIMIT)


def _split(x):
    hi = x.astype(BF16)
    lo = (x - hi.astype(F32)).astype(BF16)
    return hi, lo


def _dot(a, b):
    return jnp.dot(a, b, preferred_element_type=F32)


def _dot3(a, b):
    ah, al = _split(a)
    bh, bl = _split(b)
    return _dot(ah, bh) + _dot(al, bh) + _dot(ah, bl)


def _ada_kernel(c_ref, w_ref, b_ref, o_ref):
    c = c_ref[...]
    a = c * (1.0 / (1.0 + jnp.exp(-c)))
    o_ref[...] = _dot3(a, w_ref[...]) + b_ref[...]


def ada_mod(cond, w_ada, b_ada, tn=1024):
    depth, d, n = w_ada.shape
    rows = cond.shape[0]
    return pl.pallas_call(
        _ada_kernel,
        grid=(depth, n // tn),
        in_specs=[
            pl.BlockSpec((rows, d), lambda l, j: (0, 0)),
            pl.BlockSpec((None, d, tn), lambda l, j: (l, 0, j)),
            pl.BlockSpec((None, 1, tn), lambda l, j: (l, 0, j)),
        ],
        out_specs=pl.BlockSpec((None, rows, tn), lambda l, j: (l, 0, j)),
        out_shape=jax.ShapeDtypeStruct((depth, rows, n), F32),
        compiler_params=_cparams("parallel", "parallel"),
        name="ada_mod",
    )(cond, w_ada, b_ada.reshape(depth, 1, n))


def _mod_spec(part, row_of_token, tm):
    return pl.BlockSpec((None, 1, D_MODEL), lambda i, *_: (row_of_token(i * tm) * 6 + part, 0, 0))


def _inproj_kernel(x_ref, g_ref, sc_ref, sh_ref, w_ref, o_ref, h_ref):
    @pl.when(pl.program_id(1) == 0)
    def _():
        x = x_ref[...]
        y = x * lax.rsqrt(jnp.mean(x * x, axis=-1, keepdims=True) + EPS) * g_ref[...]
        h_ref[...] = (y * (1.0 + sc_ref[...]) + sh_ref[...]).astype(BF16)

    o_ref[...] = _dot(h_ref[...], w_ref[...]).astype(o_ref.dtype)


def in_proj(x, norm_g, mod, row_of_token, w_in, out_dtype, tm=1024, tn=1024):
    t, d = x.shape
    n = w_in.shape[1]
    tm = min(tm, t)
    return pl.pallas_call(
        _inproj_kernel,
        grid=(t // tm, n // tn),
        in_specs=[
            pl.BlockSpec((tm, d), lambda i, j: (i, 0)),
            pl.BlockSpec((1, d), lambda i, j: (0, 0)),
            _mod_spec(1, row_of_token, tm),
            _mod_spec(0, row_of_token, tm),
            pl.BlockSpec((d, tn), lambda i, j: (0, j)),
        ],
        out_specs=pl.BlockSpec((tm, tn), lambda i, j: (i, j)),
        out_shape=jax.ShapeDtypeStruct((t, n), out_dtype),
        scratch_shapes=[pltpu.VMEM((tm, d), BF16)],
        compiler_params=_cparams("parallel", "arbitrary"),
        name="in_proj",
    )(x, norm_g.reshape(1, d), mod, mod, w_in)


def _group_rms(y, group_ones):
    hi, lo = _split(y * y)
    ss = _dot(hi, group_ones) + _dot(lo, group_ones)
    return y * lax.rsqrt(ss * (1.0 / ATT_HD) + EPS)


def _rope(y, cos, sin, lane):
    swap = jnp.where((lane & (ATT_HD // 2)) == 0,
                     pltpu.roll(y, LANES - ATT_HD // 2, 1), pltpu.roll(y, ATT_HD // 2, 1))
    return y * cos + swap * sin


def _qkprep_kernel(*refs, rope, emit_state):
    if rope:
        q_ref, k_ref, gq_ref, gk_ref, cos_ref, sin_ref = refs[:6]
        outs = refs[6:]
    else:
        q_ref, k_ref, gq_ref, gk_ref = refs[:4]
        outs = refs[4:]
    qo_ref, ko_ref = outs[0], outs[1]
    tm = q_ref.shape[0]
    lane = lax.broadcasted_iota(jnp.int32, (tm, LANES), 1)
    gr = lax.broadcasted_iota(jnp.int32, (LANES, LANES), 0) // ATT_HD
    gc = lax.broadcasted_iota(jnp.int32, (LANES, LANES), 1) // ATT_HD
    group_ones = jnp.where(gr == gc, 1.0, 0.0).astype(BF16)
    for h in range(N_HEADS):
        cs = slice(h * LANES, (h + 1) * LANES)
        q = _group_rms(q_ref[:, cs].astype(F32), group_ones) * gq_ref[...]
        k = _group_rms(k_ref[:, cs].astype(F32), group_ones) * gk_ref[...]
        if emit_state:
            outs[2][:, cs] = k
        if rope:
            q = _rope(q, cos_ref[...], sin_ref[...], lane)
            k = _rope(k, cos_ref[...], sin_ref[...], lane)
        qo_ref[:, cs] = (q * Q_SCALE).astype(BF16)
        ko_ref[:, cs] = k.astype(BF16)


def qk_prep(proj, q_gain, k_gain, rope_tabs, seq_len, emit_state, tm=512):
    t = proj.shape[0]
    tm = min(tm, seq_len)
    gq = jnp.tile(q_gain.reshape(1, ATT_HD), (1, 2))
    gk = jnp.tile(k_gain.reshape(1, ATT_HD), (1, 2))
    in_specs = [
        pl.BlockSpec((tm, ATT_WIDTH), lambda i: (i, 0)),
        pl.BlockSpec((tm, ATT_WIDTH), lambda i: (i, 1)),
        pl.BlockSpec((1, LANES), lambda i: (0, 0)),
        pl.BlockSpec((1, LANES), lambda i: (0, 0)),
    ]
    args = [proj, proj, gq, gk]
    if rope_tabs is not None:
        nblk = seq_len // tm
        in_specs += [pl.BlockSpec((tm, LANES), lambda i: (i % nblk, 0))] * 2
        args += list(rope_tabs)
    out_shape = [jax.ShapeDtypeStruct((t, ATT_WIDTH), BF16)] * 2
    out_specs = [pl.BlockSpec((tm, ATT_WIDTH), lambda i: (i, 0))] * 2
    if emit_state:
        out_shape.append(jax.ShapeDtypeStruct((t, ATT_WIDTH), F32))
        out_specs.append(pl.BlockSpec((tm, ATT_WIDTH), lambda i: (i, 0)))
    return pl.pallas_call(
        functools.partial(_qkprep_kernel, rope=rope_tabs is not None, emit_state=emit_state),
        grid=(t // tm,),
        in_specs=in_specs,
        out_specs=out_specs,
        out_shape=out_shape,
        compiler_params=_cparams("parallel"),
        name="qk_prep",
    )(*args)


def rope_tables(n_tokens):
    pos = jnp.arange(n_tokens)
    rows = (pos // GRID_W).astype(F32)
    cols = (pos % GRID_W).astype(F32)
    quarter = ATT_HD // 4
    inv = ROPE_THETA ** (-jnp.arange(quarter, dtype=F32) / quarter)
    ang = jnp.concatenate([rows[:, None] * inv, cols[:, None] * inv], -1)
    cos, sin = jnp.cos(ang), jnp.sin(ang)
    return jnp.tile(cos, (1, 4)), jnp.concatenate([-sin, sin, -sin, sin], -1)


def _attn_kernel(*refs, chunks, has_ctx, lam_init):
    if has_ctx:
        q_ref, k_ref, v_ref, ck_ref, cv_ref, lw_ref, sub_ref, o_ref, s_ref, ka_ref, va_ref, m_ref, acc_ref = refs
    else:
        q_ref, k_ref, v_ref, lw_ref, sub_ref, o_ref, s_ref, ka_ref, va_ref, m_ref, acc_ref = refs
    tq = q_ref.shape[0]
    seq = k_ref.shape[0]

    @pl.when(pl.program_id(2) == 0)
    def _():
        ka_ref[0:seq, :] = k_ref[...]
        va_ref[0:seq, :] = v_ref[...].astype(BF16)
        if has_ctx:
            ka_ref[seq:, :] = ck_ref[...].astype(BF16)
            va_ref[seq:, :] = cv_ref[...].astype(BF16)

    q = q_ref[...]
    lane = lax.broadcasted_iota(jnp.int32, (tq, LANES), 1)
    zero = jnp.zeros_like(q)
    qz = jnp.concatenate([jnp.where(lane < ATT_HD, q, zero), jnp.where(lane < ATT_HD, zero, q)], axis=0)

    def scores(kc):
        return lax.dot_general(qz, kc, (((1,), (1,)), ((), ())), preferred_element_type=F32)

    def fold_max(m, s):
        for t in range(s.shape[1] // LANES):
            m = jnp.maximum(m, s[:, t * LANES:(t + 1) * LANES])
        return m

    def weighted(s, vc, mb):
        p = jnp.exp2(s - jnp.concatenate([mb] * (s.shape[1] // LANES), axis=1)).astype(BF16)
        return _dot(p, jnp.concatenate([vc, jnp.ones_like(vc)], axis=1))

    one = jnp.minimum(pl.program_id(2), 0) + 1

    def region(body):
        lax.fori_loop(0, one, lambda _, carry: body() or carry, 0)

    for n, (lo, hi) in enumerate(chunks):
        def score_chunk(n=n, lo=lo, hi=hi):
            s = scores(ka_ref[lo:hi, :])
            s_ref[:, lo:hi] = s
            m0 = jnp.full((2 * tq, LANES), -jnp.inf, F32) if n == 0 else m_ref[...]
            m = fold_max(m0, s)
            if n == len(chunks) - 1:
                m = jnp.broadcast_to(jnp.max(m, axis=-1, keepdims=True), (2 * tq, LANES))
            m_ref[...] = m

        region(score_chunk)

    for n, (lo, hi) in enumerate(chunks):
        def weigh_chunk(n=n, lo=lo, hi=hi):
            w = weighted(s_ref[:, lo:hi], va_ref[lo:hi, :], m_ref[...])
            acc_ref[...] = w if n == 0 else acc_ref[...] + w

        region(weigh_chunk)

    acc, l = acc_ref[:, :ATT_VD], acc_ref[:, ATT_VD:]

    lw = lw_ref[...]
    lam = (jnp.exp(jnp.sum(lw[0:1] * lw[1:2], axis=-1, keepdims=True))
           - jnp.exp(jnp.sum(lw[2:3] * lw[3:4], axis=-1, keepdims=True)) + lam_init)
    o = acc[:tq] / l[:tq] - lam * (acc[tq:] / l[tq:])
    o = o * lax.rsqrt(jnp.mean(o * o, axis=-1, keepdims=True) + EPS) * sub_ref[...]
    o_ref[...] = (o * (1.0 - lam_init)).astype(o_ref.dtype)


def diff_attn(qn, kn, proj, cache_k, cache_v, layer, w_lambda, subln, lam_init, batch, seq_len,
              tq=512, max_tiles=9):
    t = qn.shape[0]
    tq = min(tq, seq_len)
    nq = seq_len // tq
    has_ctx = cache_k is not None
    total = seq_len + (cache_k.shape[2] if has_ctx else 0)
    tiles = total // MXU_WIDTH
    assert tiles * MXU_WIDTH == total
    n_chunks = -(-tiles // max_tiles)
    bounds = [MXU_WIDTH * ((tiles * c + n_chunks - 1) // n_chunks) for c in range(n_chunks + 1)]
    chunks = tuple(zip(bounds[:-1], bounds[1:]))
    v_col0 = 2 * ATT_WIDTH // LANES
    in_specs = [
        pl.BlockSpec((tq, LANES), lambda b, h, i: (b * nq + i, h)),
        pl.BlockSpec((seq_len, LANES), lambda b, h, i: (b, h)),
        pl.BlockSpec((seq_len, LANES), lambda b, h, i: (b, v_col0 + h)),
    ]
    args = [qn, kn, proj]
    if has_ctx:
        past = cache_k.shape[2]
        in_specs += [pl.BlockSpec((None, None, past, LANES), lambda b, h, i: (b, layer, 0, h))] * 2
        args += [cache_k, cache_v]
    in_specs += [pl.BlockSpec((4, ATT_HD), lambda b, h, i: (0, 0)),
                 pl.BlockSpec((1, ATT_VD), lambda b, h, i: (0, 0))]
    args += [w_lambda, subln.reshape(1, ATT_VD)]
    return pl.pallas_call(
        functools.partial(_attn_kernel, chunks=chunks, has_ctx=has_ctx, lam_init=lam_init),
        grid=(batch, N_HEADS, nq),
        in_specs=in_specs,
        out_specs=pl.BlockSpec((tq, LANES), lambda b, h, i: (b * nq + i, h)),
        out_shape=jax.ShapeDtypeStruct((t, ATT_WIDTH), BF16),
        scratch_shapes=[pltpu.VMEM((2 * tq, total), F32),
                        pltpu.VMEM((total, LANES), BF16), pltpu.VMEM((total, LANES), BF16),
                        pltpu.VMEM((2 * tq, LANES), F32), pltpu.VMEM((2 * tq, 2 * ATT_VD), F32)],
        compiler_params=_cparams("parallel", "parallel", "arbitrary"),
        name="diff_attn",
    )(*args)


def _shortconv_kernel(u_ref, prev_ref, next_ref, w_ref, b_ref, v_ref, x1_ref, x2_ref, *, tiles_per_seq):
    i = pl.program_id(0)
    tl = u_ref.shape[0]
    u = u_ref[...].astype(F32)
    first = (i % tiles_per_seq) == 0
    last = (i % tiles_per_seq) == tiles_per_seq - 1
    sub = prev_ref.shape[0]
    prev_row = jnp.where(first, 0.0, prev_ref[sub - 1:sub, :].astype(F32))
    next_row = jnp.where(last, 0.0, next_ref[0:1, :].astype(F32))
    row = lax.broadcasted_iota(jnp.int32, u.shape, 0)
    below = jnp.where(row == 0, prev_row, pltpu.roll(u, 1, 0))
    above = jnp.where(row == tl - 1, next_row, pltpu.roll(u, tl - 1, 0))
    w = w_ref[...]
    y = below * w[0:1] + u * w[1:2] + above * w[2:3] + b_ref[...]
    v_ref[...] = y[:, :HY_WIDTH]
    x1_ref[...] = y[:, HY_WIDTH:2 * HY_WIDTH]
    x2_ref[...] = y[:, 2 * HY_WIDTH:]


def short_conv(proj, w_sc, b_sc, seq_len, tl=256):
    t = proj.shape[0]
    width = 3 * HY_WIDTH
    cb = 3 * ATT_WIDTH // width
    tl = min(tl, seq_len)
    sub = 16
    nsub = tl // sub
    last_sub = t // sub - 1
    part = jax.ShapeDtypeStruct((t, HY_WIDTH), F32)
    return pl.pallas_call(
        functools.partial(_shortconv_kernel, tiles_per_seq=seq_len // tl),
        grid=(t // tl,),
        in_specs=[
            pl.BlockSpec((tl, width), lambda i: (i, cb)),
            pl.BlockSpec((sub, width), lambda i: (jnp.maximum(i * nsub - 1, 0), cb)),
            pl.BlockSpec((sub, width), lambda i: (jnp.minimum((i + 1) * nsub, last_sub), cb)),
            pl.BlockSpec((3, width), lambda i: (0, 0)),
            pl.BlockSpec((1, width), lambda i: (0, 0)),
        ],
        out_specs=[pl.BlockSpec((tl, HY_WIDTH), lambda i: (i, 0))] * 3,
        out_shape=[part] * 3,
        compiler_params=_cparams("parallel"),
        name="short_conv",
    )(proj, proj, proj, w_sc, b_sc.reshape(1, width))


def _filter_kernel(z_ref, w1_ref, b1_ref, f1_ref, w2_ref, b2_ref, f2_ref, w3f_ref, w3b_ref, dl_ref,
                   o_ref, h_ref, *, seq_len):
    @pl.when(pl.program_id(0) == 0)
    def _():
        h = jnp.sin(f1_ref[...] * (_dot3(z_ref[...], w1_ref[...]) + b1_ref[...]))
        h_ref[...] = jnp.sin(f2_ref[...] * (_dot3(h, w2_ref[...]) + b2_ref[...]))

    n = 2 * seq_len
    t = z_ref[:, 0:1]
    window = jnp.exp(-t * dl_ref[...]) + HY_DECAY_SHIFT
    fwd = _dot3(h_ref[0:seq_len, :], w3f_ref[...])
    bwd = _dot3(h_ref[seq_len:n, :], w3b_ref[...])
    row = lax.broadcasted_iota(jnp.int32, (seq_len, 1), 0)
    bwd = jnp.where(row == 0, 0.0, bwd)
    fwd = fwd * window[0:seq_len]
    bwd = bwd * window[seq_len:n]
    norm = jnp.sum(jnp.abs(fwd), axis=0, keepdims=True) + jnp.sum(jnp.abs(bwd), axis=0, keepdims=True)
    o_ref[0:seq_len, :] = fwd / norm
    o_ref[seq_len:n, :] = bwd / norm


def hyena_embedding(seq_len):
    pos = jnp.concatenate([jnp.arange(seq_len), seq_len - jnp.arange(seq_len)]) % seq_len
    t = jnp.linspace(0.0, 1.0, seq_len, dtype=F32)[pos][:, None]
    wpos = (2.0 * math.pi * jnp.arange(seq_len, dtype=F32) / seq_len)[pos][:, None]
    bands = jnp.linspace(1e-4, HY_BANDS - 1, HY_BANDS, dtype=F32)[None, :]
    z = jnp.concatenate([t, jnp.cos(bands * wpos), jnp.sin(bands * wpos)], -1)
    return jnp.pad(z, ((0, 0), (0, EMB_PAD - HY_EMB)))


def hyena_filter(seq_len, w1, b1, fr1, w2, b2, fr2, w3, tn=128):
    n = 2 * seq_len
    z = hyena_embedding(seq_len)
    w1p = jnp.pad(w1, ((0, EMB_PAD - HY_EMB), (0, 0)))
    deltas = jnp.abs(jnp.linspace(HY_MIN_DECAY, HY_MAX_DECAY, HY_WIDTH, dtype=F32)).reshape(1, HY_WIDTH)
    cpo = HY_WIDTH // tn
    nblk = HY_ORDER * cpo
    row = lambda a: a.reshape(1, -1)
    full = lambda shape: pl.BlockSpec(shape, lambda j: (0,) * len(shape))
    return pl.pallas_call(
        functools.partial(_filter_kernel, seq_len=seq_len),
        grid=(nblk,),
        in_specs=[
            full((n, EMB_PAD)), full((EMB_PAD, HY_HIDDEN)), full((1, HY_HIDDEN)), full((1, HY_HIDDEN)),
            full((HY_HIDDEN, HY_HIDDEN)), full((1, HY_HIDDEN)), full((1, HY_HIDDEN)),
            pl.BlockSpec((HY_HIDDEN, tn), lambda j: (0, j)),
            pl.BlockSpec((HY_HIDDEN, tn), lambda j: (0, nblk + j)),
            pl.BlockSpec((1, tn), lambda j: (0, j % cpo)),
        ],
        out_specs=pl.BlockSpec((None, n, tn), lambda j: (j // cpo, 0, j % cpo)),
        out_shape=jax.ShapeDtypeStruct((HY_ORDER, n, HY_WIDTH), F32),
        scratch_shapes=[pltpu.VMEM((n, HY_HIDDEN), F32)],
        compiler_params=_cparams("arbitrary"),
        name="hyena_filter",
    )(z, w1p, row(b1), row(fr1), w2, row(b2), row(fr2), w3, w3, deltas)


SUB = 8
STAGE_B_MAX_UNROLL = 11
STAGE_AC_UNROLL = 2


def _chain_unroll(trips):
    return max(u for u in range(1, STAGE_B_MAX_UNROLL + 1) if trips % u == 0)


def dft_split(seq_len):
    return (64, 128) if seq_len >= 4096 else (16, 2 * seq_len // 16)


def _angles(a, b, period):
    return (2.0 * math.pi / period) * ((a * b) % period).astype(F32)


def dft_tables(seq_len):
    n1, n2 = dft_split(seq_len)
    n = n1 * n2
    kept = n1 // 2 + 1
    k1 = jnp.arange(n1)
    th = _angles(k1[:kept, None], k1[None, :], n1)
    c, s = jnp.cos(th), jnp.sin(th)
    fa = jnp.stack([c, -s], axis=1).reshape(2 * kept, n1)
    herm = jnp.where((k1[:kept] == 0) | (k1[:kept] == n1 // 2), 1.0, 2.0)[None, :, None]
    fc = jnp.stack([c.T, -s.T], axis=2)[: n1 // 2] * herm / n
    fc = fc.reshape(n1 // 2, 2 * kept)
    eye = jnp.eye(SUB, dtype=F32)
    k1 = k1[:kept]
    kk = k1[:, None, None] + n1 * jnp.arange(n2)[None, :, None]
    th = _angles(kk, jnp.arange(n2)[None, None, :], n)
    c, s = jnp.cos(th), jnp.sin(th)
    gf = jnp.concatenate([jnp.concatenate([c, s], 2), jnp.concatenate([-s, c], 2)], 1)
    ct, st = jnp.swapaxes(c, 1, 2), jnp.swapaxes(s, 1, 2)
    gi = jnp.concatenate([jnp.concatenate([ct, -st], 2), jnp.concatenate([st, ct], 2)], 1)
    return dict(n1=n1, n2=n2, kept=kept,
                fk_full=_split(jnp.kron(fa, eye)),
                fk=jnp.kron(fa[:, : n1 // 2], eye).astype(BF16),
                fki=jnp.kron(fc, eye).astype(BF16),
                gf=_split(gf), gi=gi.astype(BF16))


def _resident(shape):
    return pl.BlockSpec(shape, lambda *_: (0,) * len(shape), pipeline_mode=pl.Buffered(1))


def _spectrum_kernel(k_ref, fkh_ref, fkl_ref, gh_ref, gl_ref, o_ref, a_ref):
    rows, groups, _, tc = k_ref.shape
    n1 = gh_ref.shape[0]
    unroll = _chain_unroll(n1)

    def stage_a(g, carry):
        xh, xl = _split(k_ref[:, g].reshape(rows * SUB, tc))
        r = _dot(fkh_ref[...], xh) + _dot(fkh_ref[...], xl) + _dot(fkl_ref[...], xh)
        a_ref[:, :, g] = r.reshape(n1, 2, SUB, tc)
        return carry

    lax.fori_loop(0, groups, stage_a, 0)

    def stage_b(kk, carry):
        for u in range(unroll):
            k = kk * unroll + u
            ah, al = _split(a_ref[k].reshape(2 * groups * SUB, tc))
            o_ref[k] = _dot(gh_ref[k], ah) + _dot(gh_ref[k], al) + _dot(gl_ref[k], ah)
        return carry

    lax.fori_loop(0, n1 // unroll, stage_b, 0)


def filter_spectrum(kc, tabs, tc):
    n1, n2, kept = tabs["n1"], tabs["n2"], tabs["kept"]
    orders, n, c = kc.shape
    groups = n2 // SUB
    fkh, fkl = tabs["fk_full"]
    gh, gl = tabs["gf"]
    return pl.pallas_call(
        _spectrum_kernel,
        grid=(orders, c // tc),
        in_specs=[pl.BlockSpec((None, n1, groups, SUB, tc), lambda o, j: (o, 0, 0, 0, j)),
                  _resident(fkh.shape), _resident(fkl.shape), _resident(gh.shape), _resident(gl.shape)],
        out_specs=pl.BlockSpec((None, kept, 2 * n2, tc), lambda o, j: (o, 0, 0, j)),
        out_shape=jax.ShapeDtypeStruct((orders, kept, 2 * n2, c), F32),
        scratch_shapes=[pltpu.VMEM((kept, 2, groups, SUB, tc), F32)],
        compiler_params=_cparams("parallel", "arbitrary"),
        name="hyena_filter_spectrum",
    )(kc.reshape(orders, n1, groups, SUB, c), fkh, fkl, gh, gl)


def _hyena_conv_kernel(z_ref, x_ref, d_ref, kf_ref, fk_ref, fki_ref, gf_ref, gi_ref, o_ref, a_ref):
    rows, groups, _, tc = z_ref.shape
    n1 = gf_ref.shape[0]
    unroll = _chain_unroll(n1)
    half = groups * SUB

    def stage_a(gg, carry):
        for u in range(STAGE_AC_UNROLL):
            g = gg * STAGE_AC_UNROLL + u
            xg = z_ref[:, g].reshape(rows * SUB, tc).astype(BF16)
            a_ref[:, :, g] = _dot(fk_ref[...], xg).reshape(n1, 2, SUB, tc)
        return carry

    lax.fori_loop(0, groups // STAGE_AC_UNROLL, stage_a, 0)

    def stage_b(kk, carry):
        for u in range(unroll):
            k = kk * unroll + u
            x = _dot(gf_ref[k], a_ref[k].reshape(2 * half, tc).astype(BF16))
            kf = kf_ref[k]
            xr, xi = x[:half], x[half:]
            kr, ki = kf[:half], kf[half:]
            p = jnp.concatenate([xr * kr - xi * ki, xr * ki + xi * kr], axis=0).astype(BF16)
            a_ref[k] = _dot(gi_ref[k], p).reshape(2, groups, SUB, tc)
        return carry

    lax.fori_loop(0, n1 // unroll, stage_b, 0)

    d = d_ref[...].reshape(1, 1, tc)

    def stage_c(gg, carry):
        for u in range(STAGE_AC_UNROLL):
            g = gg * STAGE_AC_UNROLL + u
            bg = a_ref[:, :, g].reshape(n1 * 2 * SUB, tc).astype(BF16)
            y = _dot(fki_ref[...], bg).reshape(rows, SUB, tc)
            o_ref[:, g] = x_ref[:, g] * (y + z_ref[:, g] * d)
        return carry

    lax.fori_loop(0, groups // STAGE_AC_UNROLL, stage_c, 0)


def hyena_conv(z, x, d, kf, order, tabs, tc):
    n1, n2, kept = tabs["n1"], tabs["n2"], tabs["kept"]
    b, seq_len, c = z.shape
    rows, groups = n1 // 2, n2 // SUB
    view = lambda a: a.reshape(b, rows, groups, SUB, c)
    seq_spec = pl.BlockSpec((None, rows, groups, SUB, tc), lambda j, i: (i, 0, 0, 0, j))
    gf = tabs["gf"][0]
    out = pl.pallas_call(
        _hyena_conv_kernel,
        grid=(c // tc, b),
        in_specs=[seq_spec, seq_spec,
                  pl.BlockSpec((1, tc), lambda j, i: (0, j)),
                  pl.BlockSpec((None, kept, 2 * n2, tc), lambda j, i: (order, 0, 0, j),
                               pipeline_mode=pl.Buffered(1)),
                  _resident(tabs["fk"].shape), _resident(tabs["fki"].shape),
                  _resident(gf.shape), _resident(tabs["gi"].shape)],
        out_specs=seq_spec,
        out_shape=jax.ShapeDtypeStruct((b, rows, groups, SUB, c), F32),
        scratch_shapes=[pltpu.VMEM((kept, 2, groups, SUB, tc), F32)],
        compiler_params=_cparams("parallel", "arbitrary"),
        name="hyena_conv",
    )(view(z), view(x), d.reshape(1, c), kf, tabs["fk"], tabs["fki"], gf, tabs["gi"])
    return out.reshape(b, seq_len, c)


def hyena_mixer(proj, batch, seq_len, tabs, w_sc, b_sc, w1, b1, fr1, w2, b2, fr2, w3, hy_d):
    c = HY_WIDTH
    kc = hyena_filter(seq_len, w1, b1, fr1, w2, b2, fr2, w3)
    tc = 2 * LANES if seq_len >= 4096 else c
    kf = filter_spectrum(kc, tabs, tc)
    v, x1, x2 = short_conv(proj, w_sc, b_sc, seq_len)
    z = v.reshape(batch, seq_len, c)
    for o, x in enumerate((x1, x2)):
        z = hyena_conv(z, x.reshape(batch, seq_len, c), hy_d[o], kf, o, tabs, tc)
    return z.reshape(batch * seq_len, c)


def _pool_kernel(u_ref, w_ref, s_ref, o_ref, pad_ref):
    seq_len = u_ref.shape[0]
    h = POOL_HALO
    zeros = jnp.zeros((h, POOL_WIDTH), F32)
    pad_ref[0:h, :] = zeros
    pad_ref[h + seq_len:h + seq_len + h, :] = zeros
    pad_ref[h:h + seq_len, :] = u_ref[...].astype(F32)
    t = lax.broadcasted_iota(jnp.int32, (seq_len, 1), 0)
    for g, w in enumerate(POOL_WINDOWS):
        cs = slice(g * POOL_GW, (g + 1) * POOL_GW)
        tot = pad_ref[h - w // 2:h - w // 2 + seq_len, cs]
        for d in range(1 - w // 2, w - w // 2):
            tot = tot + pad_ref[h + d:h + d + seq_len, cs]
        cnt = (jnp.minimum(t - w // 2 + w, seq_len) - jnp.maximum(t - w // 2, 0)).astype(F32)
        pooled = tot / cnt - pad_ref[h:h + seq_len, cs]
        o_ref[:, cs] = (_dot(pooled.astype(BF16), w_ref[g]) * s_ref[:, cs]).astype(o_ref.dtype)


def pool_mixer(proj, w_pool, pool_scale, batch, seq_len):
    t = proj.shape[0]
    cb = (IN_COLS - POOL_WIDTH) // POOL_WIDTH
    return pl.pallas_call(
        _pool_kernel,
        grid=(batch,),
        in_specs=[pl.BlockSpec((seq_len, POOL_WIDTH), lambda b: (b, cb)),
                  pl.BlockSpec((len(POOL_WINDOWS), POOL_GW, POOL_GW), lambda b: (0, 0, 0)),
                  pl.BlockSpec((1, POOL_WIDTH), lambda b: (0, 0))],
        out_specs=pl.BlockSpec((seq_len, POOL_WIDTH), lambda b: (b, 0)),
        out_shape=jax.ShapeDtypeStruct((t, POOL_WIDTH), BF16),
        scratch_shapes=[pltpu.VMEM((seq_len + 2 * POOL_HALO, POOL_WIDTH), F32)],
        compiler_params=_cparams("parallel"),
        name="pool_mixer",
    )(proj, w_pool.astype(BF16), pool_scale.reshape(1, POOL_WIDTH))


def _outproj_kernel(att_ref, hy_ref, po_ref, x_ref, wa_ref, wh_ref, wp_ref, g1_ref, g_ref, sc_ref, sh_ref,
                    x1_ref, h2_ref):
    mix = (_dot(att_ref[...], wa_ref[...]) + _dot(hy_ref[...].astype(BF16), wh_ref[...])
           + _dot(po_ref[...], wp_ref[...]))
    x1 = x_ref[...] + g1_ref[...] * mix
    x1_ref[...] = x1
    y = x1 * lax.rsqrt(jnp.mean(x1 * x1, axis=-1, keepdims=True) + EPS) * g_ref[...]
    h2_ref[...] = (y * (1.0 + sc_ref[...]) + sh_ref[...]).astype(BF16)


def out_proj(att, hy, po, x, w_out, norm_g, mod, row_of_token, tm=512):
    t, d = x.shape
    a0, a1 = ATT_WIDTH, ATT_WIDTH + HY_WIDTH
    const = lambda shape: pl.BlockSpec(shape, lambda i: (0, 0))
    return pl.pallas_call(
        _outproj_kernel,
        grid=(t // tm,),
        in_specs=[
            pl.BlockSpec((tm, ATT_WIDTH), lambda i: (i, 0)),
            pl.BlockSpec((tm, HY_WIDTH), lambda i: (i, 0)),
            pl.BlockSpec((tm, POOL_WIDTH), lambda i: (i, 0)),
            pl.BlockSpec((tm, d), lambda i: (i, 0)),
            const((ATT_WIDTH, d)),
            pl.BlockSpec((HY_WIDTH, d), lambda i: (a0 // HY_WIDTH, 0)),
            pl.BlockSpec((POOL_WIDTH, d), lambda i: (a1 // POOL_WIDTH, 0)),
            _mod_spec(2, row_of_token, tm),
            const((1, d)),
            _mod_spec(4, row_of_token, tm),
            _mod_spec(3, row_of_token, tm),
        ],
        out_specs=[pl.BlockSpec((tm, d), lambda i: (i, 0))] * 2,
        out_shape=[jax.ShapeDtypeStruct((t, d), F32), jax.ShapeDtypeStruct((t, d), BF16)],
        compiler_params=_cparams("parallel"),
        name="out_proj",
    )(att, hy, po, x, w_out, w_out, w_out, mod, norm_g.reshape(1, d), mod, mod)


def _ffn_kernel(h_ref, x_ref, wg_ref, wu_ref, wd_ref, g2_ref, o_ref, acc_ref):
    j = pl.program_id(1)

    @pl.when(j == 0)
    def _():
        acc_ref[...] = jnp.zeros_like(acc_ref)

    h = h_ref[...]
    gate = _dot(h, wg_ref[...])
    up = _dot(h, wu_ref[...])
    act = (gate * (1.0 / (1.0 + jnp.exp(-gate))) * up).astype(BF16)
    acc_ref[...] += _dot(act, wd_ref[...])

    @pl.when(j == pl.num_programs(1) - 1)
    def _():
        o_ref[...] = x_ref[...] + g2_ref[...] * acc_ref[...]


def ffn(h2, x1, w_gate, w_up, w_down, mod, row_of_token, tm=512, th=512):
    t, d = x1.shape
    hidden = w_gate.shape[1]
    tm = min(tm, t)
    return pl.pallas_call(
        _ffn_kernel,
        grid=(t // tm, hidden // th),
        in_specs=[
            pl.BlockSpec((tm, d), lambda i, j: (i, 0)),
            pl.BlockSpec((tm, d), lambda i, j: (i, 0)),
            pl.BlockSpec((d, th), lambda i, j: (0, j)),
            pl.BlockSpec((d, th), lambda i, j: (0, j)),
            pl.BlockSpec((th, d), lambda i, j: (j, 0)),
            _mod_spec(5, row_of_token, tm),
        ],
        out_specs=pl.BlockSpec((tm, d), lambda i, j: (i, 0)),
        out_shape=jax.ShapeDtypeStruct((t, d), F32),
        scratch_shapes=[pltpu.VMEM((tm, d), F32)],
        compiler_params=_cparams("parallel", "arbitrary"),
        name="ffn",
    )(h2, x1, w_gate, w_up, w_down, mod)


def trunk_layer(x, batch, seq_len, mod, row_of_token, rope_tabs, cache, layer, lam_init, tabs, p, emit_state):
    proj = in_proj(x, p["norm1"], mod, row_of_token, p["w_in"], F32 if emit_state else BF16)
    prep = qk_prep(proj, p["q_norm"], p["k_norm"], rope_tabs, seq_len, emit_state)
    cache_k, cache_v = cache if cache is not None else (None, None)
    att = diff_attn(prep[0], prep[1], proj, cache_k, cache_v, layer, p["w_lambda"], p["subln"], lam_init,
                    batch, seq_len)
    hy = hyena_mixer(proj, batch, seq_len, tabs, p["w_sc"], p["b_sc"], p["hy_w1"], p["hy_b1"], p["hy_fr1"],
                     p["hy_w2"], p["hy_b2"], p["hy_fr2"], p["hy_w3"], p["hy_d"])
    po = pool_mixer(proj, p["w_pool"], p["pool_scale"], batch, seq_len)
    x1, h2 = out_proj(att, hy, po, x, p["w_out"], p["norm2"], mod, row_of_token)
    y = ffn(h2, x1, p["w_gate"], p["w_up"], p["w_down"], mod, row_of_token)
    if emit_state:
        return y, prep[2], proj[:, 2 * ATT_WIDTH:3 * ATT_WIDTH]
    return y, None, None


def kernel(x_prompt, x_sample, cache_k, cache_v, c, c_ctx, w_ada, b_ada, norm1, norm2, w_in, q_norm, k_norm, w_lambda, subln, w_sc, b_sc, hy_w1, hy_b1, hy_fr1, hy_w2, hy_b2, hy_fr2, hy_w3, hy_d, w_pool, pool_scale, w_out, w_gate, w_up, w_down):
    batch, seq, d = x_prompt.shape
    dec_batch, dec_seq, _ = x_sample.shape
    depth = w_ada.shape[0]
    past = cache_k.shape[2]

    cond = jnp.concatenate([c_ctx[None, :], c, jnp.zeros((MOD_ROWS - 1 - dec_batch, d), F32)], axis=0)
    mod_all = ada_mod(cond, w_ada, b_ada)
    rope_tabs = rope_tables(dec_seq)
    tabs_ctx, tabs_lat = dft_tables(seq), dft_tables(dec_seq)
    ck = cache_k.reshape(dec_batch, depth, past, ATT_WIDTH)
    cv = cache_v.reshape(dec_batch, depth, past, ATT_WIDTH)

    y_p = x_prompt.reshape(batch * seq, d)
    y_s = x_sample.reshape(dec_batch * dec_seq, d)
    ks, vs = [], []
    for l in range(depth):
        p = dict(norm1=norm1[l], norm2=norm2[l], w_in=w_in[l].astype(BF16), q_norm=q_norm[l], k_norm=k_norm[l],
                 w_lambda=w_lambda[l], subln=subln[l], w_sc=w_sc[l], b_sc=b_sc[l], hy_w1=hy_w1[l],
                 hy_b1=hy_b1[l], hy_fr1=hy_fr1[l], hy_w2=hy_w2[l], hy_b2=hy_b2[l], hy_fr2=hy_fr2[l],
                 hy_w3=hy_w3[l], hy_d=hy_d[l], w_pool=w_pool[l], pool_scale=pool_scale[l],
                 w_out=w_out[l].astype(BF16), w_gate=w_gate[l].astype(BF16), w_up=w_up[l].astype(BF16),
                 w_down=w_down[l].astype(BF16))
        lam_init = 0.8 - 0.6 * math.exp(-0.3 * l)
        mod = mod_all[l].reshape(MOD_ROWS * 6, 1, d)
        y_p, k_l, v_l = trunk_layer(y_p, batch, seq, mod, lambda tok: 0, None, None, l, lam_init,
                                    tabs_ctx, p, True)
        ks.append(k_l.reshape(batch, seq, N_HEADS, 2, ATT_HD))
        vs.append(v_l.reshape(batch, seq, N_HEADS, ATT_VD))
        y_s, _, _ = trunk_layer(y_s, dec_batch, dec_seq, mod, lambda tok: 1 + tok // dec_seq, rope_tabs,
                                (ck, cv), l, lam_init, tabs_lat, p, False)
    return (y_p.reshape(batch, seq, d), y_s.reshape(dec_batch, dec_seq, d),
            jnp.stack(ks, axis=1), jnp.stack(vs, axis=1))
```

```python
import functools
import math

import jax
import jax.numpy as jnp
from jax import lax
from jax.experimental import pallas as pl
from jax.experimental.pallas import tpu as pltpu

F32 = jnp.float32
BF16 = jnp.bfloat16

D_MODEL = 2048
DEPTH = 2
GRID_W = 64
ATT_WIDTH = 1024
HY_WIDTH = 512
POOL_WIDTH = 512
ATT_HD = 64
ATT_VD = 128
N_HEADS = 8
ROPE_THETA = 10000.0
HY_ORDER = 2
HY_BANDS = 16
HY_EMB = 1 + 2 * HY_BANDS
HY_HIDDEN = 64
HY_DECAY_TARGET = 1e-2
HY_FAST_DECAY = 0.3
HY_SLOW_DECAY = 1.5
HY_DECAY_SHIFT = 0.05
HY_MIN_DECAY = math.log(HY_DECAY_TARGET) / HY_SLOW_DECAY
HY_MAX_DECAY = math.log(HY_DECAY_TARGET) / HY_FAST_DECAY
POOL_WINDOWS = (2, 4, 8, 16)
POOL_GW = 128
POOL_HALO = 16
FFN_HIDDEN = 5632
IN_COLS = 5120
EPS = 1e-6
Q_SCALE = ATT_HD ** -0.5 * math.log2(math.e)
MOD_ROWS = 16
EMB_PAD = 128
LANES = 128
MXU_WIDTH = 256
VMEM_LIMIT = 56 * 1024 * 1024


def _cparams(*sem):
    return pltpu.CompilerParams(dimension_semantics=sem, vmem_limit_bytes=VMEM_LIMIT)


def _split(x):
    hi = x.astype(BF16)
    lo = (x - hi.astype(F32)).astype(BF16)
    return hi, lo


def _dot(a, b):
    return jnp.dot(a, b, preferred_element_type=F32)


def _dot3(a, b):
    ah, al = _split(a)
    bh, bl = _split(b)
    return _dot(ah, bh) + _dot(al, bh) + _dot(ah, bl)


def _ada_kernel(c_ref, w_ref, b_ref, o_ref):
    c = c_ref[...]
    a = c * (1.0 / (1.0 + jnp.exp(-c)))
    o_ref[...] = _dot3(a, w_ref[...]) + b_ref[...]


def ada_mod(cond, w_ada, b_ada, tn=1024):
    depth, d, n = w_ada.shape
    rows = cond.shape[0]
    return pl.pallas_call(
        _ada_kernel,
        grid=(depth, n // tn),
        in_specs=[
            pl.BlockSpec((rows, d), lambda l, j: (0, 0)),
            pl.BlockSpec((None, d, tn), lambda l, j: (l, 0, j)),
            pl.BlockSpec((None, 1, tn), lambda l, j: (l, 0, j)),
        ],
        out_specs=pl.BlockSpec((None, rows, tn), lambda l, j: (l, 0, j)),
        out_shape=jax.ShapeDtypeStruct((depth, rows, n), F32),
        compiler_params=_cparams("parallel", "parallel"),
        name="ada_mod",
    )(cond, w_ada, b_ada.reshape(depth, 1, n))


def _mod_spec(part, row_of_token, tm):
    return pl.BlockSpec((None, 1, D_MODEL), lambda i, *_: (row_of_token(i * tm) * 6 + part, 0, 0))


def _inproj_kernel(x_ref, g_ref, sc_ref, sh_ref, w_ref, o_ref, h_ref):
    @pl.when(pl.program_id(1) == 0)
    def _():
        x = x_ref[...]
        y = x * lax.rsqrt(jnp.mean(x * x, axis=-1, keepdims=True) + EPS) * g_ref[...]
        h_ref[...] = (y * (1.0 + sc_ref[...]) + sh_ref[...]).astype(BF16)

    o_ref[...] = _dot(h_ref[...], w_ref[...]).astype(o_ref.dtype)


def in_proj(x, norm_g, mod, row_of_token, w_in, out_dtype, tm=1024, tn=1280):
    t, d = x.shape
    n = w_in.shape[1]
    tm = min(tm, t)
    return pl.pallas_call(
        _inproj_kernel,
        grid=(t // tm, n // tn),
        in_specs=[
            pl.BlockSpec((tm, d), lambda i, j: (i, 0)),
            pl.BlockSpec((1, d), lambda i, j: (0, 0)),
            _mod_spec(1, row_of_token, tm),
            _mod_spec(0, row_of_token, tm),
            pl.BlockSpec((d, tn), lambda i, j: (0, j)),
        ],
        out_specs=pl.BlockSpec((tm, tn), lambda i, j: (i, j)),
        out_shape=jax.ShapeDtypeStruct((t, n), out_dtype),
        scratch_shapes=[pltpu.VMEM((tm, d), BF16)],
        compiler_params=_cparams("parallel", "arbitrary"),
        name="in_proj",
    )(x, norm_g.reshape(1, d), mod, mod, w_in)


def _group_rms(y, group_ones):
    hi, lo = _split(y * y)
    ss = _dot(hi, group_ones) + _dot(lo, group_ones)
    return y * lax.rsqrt(ss * (1.0 / ATT_HD) + EPS)


def _rope(y, cos, sin, lane):
    swap = jnp.where((lane & (ATT_HD // 2)) == 0,
                     pltpu.roll(y, LANES - ATT_HD // 2, 1), pltpu.roll(y, ATT_HD // 2, 1))
    return y * cos + swap * sin


def _qkprep_kernel(*refs, rope, emit_state):
    if rope:
        q_ref, k_ref, gq_ref, gk_ref, cos_ref, sin_ref = refs[:6]
        outs = refs[6:]
    else:
        q_ref, k_ref, gq_ref, gk_ref = refs[:4]
        outs = refs[4:]
    qo_ref, ko_ref = outs[0], outs[1]
    tm = q_ref.shape[0]
    lane = lax.broadcasted_iota(jnp.int32, (tm, LANES), 1)
    gr = lax.broadcasted_iota(jnp.int32, (LANES, LANES), 0) // ATT_HD
    gc = lax.broadcasted_iota(jnp.int32, (LANES, LANES), 1) // ATT_HD
    group_ones = jnp.where(gr == gc, 1.0, 0.0).astype(BF16)
    for h in range(N_HEADS):
        cs = slice(h * LANES, (h + 1) * LANES)
        q = _group_rms(q_ref[:, cs].astype(F32), group_ones) * gq_ref[...]
        k = _group_rms(k_ref[:, cs].astype(F32), group_ones) * gk_ref[...]
        if emit_state:
            outs[2][:, cs] = k
        if rope:
            q = _rope(q, cos_ref[...], sin_ref[...], lane)
            k = _rope(k, cos_ref[...], sin_ref[...], lane)
        qo_ref[:, cs] = (q * Q_SCALE).astype(BF16)
        ko_ref[:, cs] = k.astype(BF16)


def qk_prep(proj, q_gain, k_gain, rope_tabs, seq_len, emit_state, tm=512):
    t = proj.shape[0]
    tm = min(tm, seq_len)
    gq = jnp.tile(q_gain.reshape(1, ATT_HD), (1, 2))
    gk = jnp.tile(k_gain.reshape(1, ATT_HD), (1, 2))
    in_specs = [
        pl.BlockSpec((tm, ATT_WIDTH), lambda i: (i, 0)),
        pl.BlockSpec((tm, ATT_WIDTH), lambda i: (i, 1)),
        pl.BlockSpec((1, LANES), lambda i: (0, 0)),
        pl.BlockSpec((1, LANES), lambda i: (0, 0)),
    ]
    args = [proj, proj, gq, gk]
    if rope_tabs is not None:
        nblk = seq_len // tm
        in_specs += [pl.BlockSpec((tm, LANES), lambda i: (i % nblk, 0))] * 2
        args += list(rope_tabs)
    out_shape = [jax.ShapeDtypeStruct((t, ATT_WIDTH), BF16)] * 2
    out_specs = [pl.BlockSpec((tm, ATT_WIDTH), lambda i: (i, 0))] * 2
    if emit_state:
        out_shape.append(jax.ShapeDtypeStruct((t, ATT_WIDTH), F32))
        out_specs.append(pl.BlockSpec((tm, ATT_WIDTH), lambda i: (i, 0)))
    return pl.pallas_call(
        functools.partial(_qkprep_kernel, rope=rope_tabs is not None, emit_state=emit_state),
        grid=(t // tm,),
        in_specs=in_specs,
        out_specs=out_specs,
        out_shape=out_shape,
        compiler_params=_cparams("parallel"),
        name="qk_prep",
    )(*args)


def rope_tables(n_tokens):
    pos = jnp.arange(n_tokens)
    rows = (pos // GRID_W).astype(F32)
    cols = (pos % GRID_W).astype(F32)
    quarter = ATT_HD // 4
    inv = ROPE_THETA ** (-jnp.arange(quarter, dtype=F32) / quarter)
    ang = jnp.concatenate([rows[:, None] * inv, cols[:, None] * inv], -1)
    cos, sin = jnp.cos(ang), jnp.sin(ang)
    return jnp.tile(cos, (1, 4)), jnp.concatenate([-sin, sin, -sin, sin], -1)


def _attn_kernel(*refs, chunks, has_ctx, lam_init):
    if has_ctx:
        q_ref, k_ref, v_ref, ck_ref, cv_ref, lw_ref, sub_ref, o_ref, s_ref, ka_ref, va_ref, m_ref, acc_ref = refs
    else:
        q_ref, k_ref, v_ref, lw_ref, sub_ref, o_ref, s_ref, ka_ref, va_ref, m_ref, acc_ref = refs
    tq = q_ref.shape[0]
    seq = k_ref.shape[0]

    @pl.when(pl.program_id(2) == 0)
    def _():
        ka_ref[0:seq, :] = k_ref[...]
        va_ref[0:seq, :] = v_ref[...].astype(BF16)
        if has_ctx:
            ka_ref[seq:, :] = ck_ref[...].astype(BF16)
            va_ref[seq:, :] = cv_ref[...].astype(BF16)

    q = q_ref[...]
    lane = lax.broadcasted_iota(jnp.int32, (tq, LANES), 1)
    zero = jnp.zeros_like(q)
    qz = jnp.concatenate([jnp.where(lane < ATT_HD, q, zero), jnp.where(lane < ATT_HD, zero, q)], axis=0)

    def scores(kc):
        return lax.dot_general(qz, kc, (((1,), (1,)), ((), ())), preferred_element_type=F32)

    def fold_max(m, s):
        for t in range(s.shape[1] // LANES):
            m = jnp.maximum(m, s[:, t * LANES:(t + 1) * LANES])
        return m

    def weighted(s, vc, mb):
        p = jnp.exp2(s - jnp.concatenate([mb] * (s.shape[1] // LANES), axis=1)).astype(BF16)
        return _dot(p, jnp.concatenate([vc, jnp.ones_like(vc)], axis=1))

    one = jnp.minimum(pl.program_id(2), 0) + 1

    def region(body):
        lax.fori_loop(0, one, lambda _, carry: body() or carry, 0)

    for n, (lo, hi) in enumerate(chunks):
        def score_chunk(n=n, lo=lo, hi=hi):
            s = scores(ka_ref[lo:hi, :])
            s_ref[:, lo:hi] = s
            m0 = jnp.full((2 * tq, LANES), -jnp.inf, F32) if n == 0 else m_ref[...]
            m = fold_max(m0, s)
            if n == len(chunks) - 1:
                m = jnp.broadcast_to(jnp.max(m, axis=-1, keepdims=True), (2 * tq, LANES))
            m_ref[...] = m

        region(score_chunk)

    for n, (lo, hi) in enumerate(chunks):
        def weigh_chunk(n=n, lo=lo, hi=hi):
            w = weighted(s_ref[:, lo:hi], va_ref[lo:hi, :], m_ref[...])
            acc_ref[...] = w if n == 0 else acc_ref[...] + w

        region(weigh_chunk)

    acc, l = acc_ref[:, :ATT_VD], acc_ref[:, ATT_VD:]

    lw = lw_ref[...]
    lam = (jnp.exp(jnp.sum(lw[0:1] * lw[1:2], axis=-1, keepdims=True))
           - jnp.exp(jnp.sum(lw[2:3] * lw[3:4], axis=-1, keepdims=True)) + lam_init)
    o = acc[:tq] / l[:tq] - lam * (acc[tq:] / l[tq:])
    o = o * lax.rsqrt(jnp.mean(o * o, axis=-1, keepdims=True) + EPS) * sub_ref[...]
    o_ref[...] = (o * (1.0 - lam_init)).astype(o_ref.dtype)


def diff_attn(qn, kn, proj, cache_k, cache_v, layer, w_lambda, subln, lam_init, batch, seq_len,
              tq=512, max_tiles=9):
    t = qn.shape[0]
    tq = min(tq, seq_len)
    nq = seq_len // tq
    has_ctx = cache_k is not None
    total = seq_len + (cache_k.shape[2] if has_ctx else 0)
    tiles = total // MXU_WIDTH
    assert tiles * MXU_WIDTH == total
    n_chunks = -(-tiles // max_tiles)
    bounds = [MXU_WIDTH * ((tiles * c + n_chunks - 1) // n_chunks) for c in range(n_chunks + 1)]
    chunks = tuple(zip(bounds[:-1], bounds[1:]))
    v_col0 = 2 * ATT_WIDTH // LANES
    in_specs = [
        pl.BlockSpec((tq, LANES), lambda b, h, i: (b * nq + i, h)),
        pl.BlockSpec((seq_len, LANES), lambda b, h, i: (b, h)),
        pl.BlockSpec((seq_len, LANES), lambda b, h, i: (b, v_col0 + h)),
    ]
    args = [qn, kn, proj]
    if has_ctx:
        past = cache_k.shape[2]
        in_specs += [pl.BlockSpec((None, None, past, LANES), lambda b, h, i: (b, layer, 0, h))] * 2
        args += [cache_k, cache_v]
    in_specs += [pl.BlockSpec((4, ATT_HD), lambda b, h, i: (0, 0)),
                 pl.BlockSpec((1, ATT_VD), lambda b, h, i: (0, 0))]
    args += [w_lambda, subln.reshape(1, ATT_VD)]
    return pl.pallas_call(
        functools.partial(_attn_kernel, chunks=chunks, has_ctx=has_ctx, lam_init=lam_init),
        grid=(batch, N_HEADS, nq),
        in_specs=in_specs,
        out_specs=pl.BlockSpec((tq, LANES), lambda b, h, i: (b * nq + i, h)),
        out_shape=jax.ShapeDtypeStruct((t, ATT_WIDTH), BF16),
        scratch_shapes=[pltpu.VMEM((2 * tq, total), F32),
                        pltpu.VMEM((total, LANES), BF16), pltpu.VMEM((total, LANES), BF16),
                        pltpu.VMEM((2 * tq, LANES), F32), pltpu.VMEM((2 * tq, 2 * ATT_VD), F32)],
        compiler_params=_cparams("parallel", "parallel", "arbitrary"),
        name="diff_attn",
    )(*args)


def _shortconv_kernel(u_ref, prev_ref, next_ref, w_ref, b_ref, v_ref, x1_ref, x2_ref, *, tiles_per_seq):
    i = pl.program_id(0)
    tl = u_ref.shape[0]
    u = u_ref[...].astype(F32)
    first = (i % tiles_per_seq) == 0
    last = (i % tiles_per_seq) == tiles_per_seq - 1
    sub = prev_ref.shape[0]
    prev_row = jnp.where(first, 0.0, prev_ref[sub - 1:sub, :].astype(F32))
    next_row = jnp.where(last, 0.0, next_ref[0:1, :].astype(F32))
    row = lax.broadcasted_iota(jnp.int32, u.shape, 0)
    below = jnp.where(row == 0, prev_row, pltpu.roll(u, 1, 0))
    above = jnp.where(row == tl - 1, next_row, pltpu.roll(u, tl - 1, 0))
    w = w_ref[...]
    y = below * w[0:1] + u * w[1:2] + above * w[2:3] + b_ref[...]
    v_ref[...] = y[:, :HY_WIDTH]
    x1_ref[...] = y[:, HY_WIDTH:2 * HY_WIDTH]
    x2_ref[...] = y[:, 2 * HY_WIDTH:]


def short_conv(proj, w_sc, b_sc, seq_len, tl=512):
    t = proj.shape[0]
    width = 3 * HY_WIDTH
    cb = 3 * ATT_WIDTH // width
    tl = min(tl, seq_len)
    sub = 16
    nsub = tl // sub
    last_sub = t // sub - 1
    part = jax.ShapeDtypeStruct((t, HY_WIDTH), F32)
    return pl.pallas_call(
        functools.partial(_shortconv_kernel, tiles_per_seq=seq_len // tl),
        grid=(t // tl,),
        in_specs=[
            pl.BlockSpec((tl, width), lambda i: (i, cb)),
            pl.BlockSpec((sub, width), lambda i: (jnp.maximum(i * nsub - 1, 0), cb)),
            pl.BlockSpec((sub, width), lambda i: (jnp.minimum((i + 1) * nsub, last_sub), cb)),
            pl.BlockSpec((3, width), lambda i: (0, 0)),
            pl.BlockSpec((1, width), lambda i: (0, 0)),
        ],
        out_specs=[pl.BlockSpec((tl, HY_WIDTH), lambda i: (i, 0))] * 3,
        out_shape=[part] * 3,
        compiler_params=_cparams("parallel"),
        name="short_conv",
    )(proj, proj, proj, w_sc, b_sc.reshape(1, width))


def _filter_kernel(z_ref, w1_ref, b1_ref, f1_ref, w2_ref, b2_ref, f2_ref, w3f_ref, w3b_ref, dl_ref,
                   o_ref, h_ref, *, seq_len):
    @pl.when(pl.program_id(0) == 0)
    def _():
        h = jnp.sin(f1_ref[...] * (_dot3(z_ref[...], w1_ref[...]) + b1_ref[...]))
        h_ref[...] = jnp.sin(f2_ref[...] * (_dot3(h, w2_ref[...]) + b2_ref[...]))

    n = 2 * seq_len
    t = z_ref[:, 0:1]
    window = jnp.exp(-t * dl_ref[...]) + HY_DECAY_SHIFT
    fwd = _dot3(h_ref[0:seq_len, :], w3f_ref[...])
    bwd = _dot3(h_ref[seq_len:n, :], w3b_ref[...])
    row = lax.broadcasted_iota(jnp.int32, (seq_len, 1), 0)
    bwd = jnp.where(row == 0, 0.0, bwd)
    fwd = fwd * window[0:seq_len]
    bwd = bwd * window[seq_len:n]
    norm = jnp.sum(jnp.abs(fwd), axis=0, keepdims=True) + jnp.sum(jnp.abs(bwd), axis=0, keepdims=True)
    o_ref[0:seq_len, :] = fwd / norm
    o_ref[seq_len:n, :] = bwd / norm


def hyena_embedding(seq_len):
    t = jnp.linspace(0.0, 1.0, seq_len, dtype=F32)[:, None]
    wpos = (2.0 * math.pi * jnp.arange(seq_len, dtype=F32) / seq_len)[:, None]
    bands = jnp.linspace(1e-4, HY_BANDS - 1, HY_BANDS, dtype=F32)[None, :]
    z = jnp.concatenate([t, jnp.cos(bands * wpos), jnp.sin(bands * wpos)], -1)
    z = jnp.concatenate([z, z[:1], z[1:][::-1]], axis=0)
    return jnp.pad(z, ((0, 0), (0, EMB_PAD - HY_EMB)))


def hyena_filter(seq_len, w1, b1, fr1, w2, b2, fr2, w3, tn=128):
    n = 2 * seq_len
    z = hyena_embedding(seq_len)
    w1p = jnp.pad(w1, ((0, EMB_PAD - HY_EMB), (0, 0)))
    deltas = jnp.abs(jnp.linspace(HY_MIN_DECAY, HY_MAX_DECAY, HY_WIDTH, dtype=F32)).reshape(1, HY_WIDTH)
    cpo = HY_WIDTH // tn
    nblk = HY_ORDER * cpo
    row = lambda a: a.reshape(1, -1)
    full = lambda shape: pl.BlockSpec(shape, lambda j: (0,) * len(shape))
    return pl.pallas_call(
        functools.partial(_filter_kernel, seq_len=seq_len),
        grid=(nblk,),
        in_specs=[
            full((n, EMB_PAD)), full((EMB_PAD, HY_HIDDEN)), full((1, HY_HIDDEN)), full((1, HY_HIDDEN)),
            full((HY_HIDDEN, HY_HIDDEN)), full((1, HY_HIDDEN)), full((1, HY_HIDDEN)),
            pl.BlockSpec((HY_HIDDEN, tn), lambda j: (0, j)),
            pl.BlockSpec((HY_HIDDEN, tn), lambda j: (0, nblk + j)),
            pl.BlockSpec((1, tn), lambda j: (0, j % cpo)),
        ],
        out_specs=pl.BlockSpec((None, n, tn), lambda j: (j // cpo, 0, j % cpo)),
        out_shape=jax.ShapeDtypeStruct((HY_ORDER, n, HY_WIDTH), F32),
        scratch_shapes=[pltpu.VMEM((n, HY_HIDDEN), F32)],
        compiler_params=_cparams("arbitrary"),
        name="hyena_filter",
    )(z, w1p, row(b1), row(fr1), w2, row(b2), row(fr2), w3, w3, deltas)


SUB = 8
STAGE_B_MAX_UNROLL = 11
STAGE_AC_UNROLL = 2


def _chain_unroll(trips):
    return max(u for u in range(1, STAGE_B_MAX_UNROLL + 1) if trips % u == 0)


def dft_split(seq_len):
    return (64, 128) if seq_len >= 4096 else (16, 2 * seq_len // 16)


def _angles(a, b, period):
    return (2.0 * math.pi / period) * ((a * b) % period).astype(F32)


def dft_tables(seq_len):
    n1, n2 = dft_split(seq_len)
    n = n1 * n2
    kept = n1 // 2 + 1
    k1 = jnp.arange(n1)
    th = _angles(k1[:kept, None], k1[None, :], n1)
    c, s = jnp.cos(th), jnp.sin(th)
    fa = jnp.stack([c, -s], axis=1).reshape(2 * kept, n1)
    herm = jnp.where((k1[:kept] == 0) | (k1[:kept] == n1 // 2), 1.0, 2.0)[None, :, None]
    fc = jnp.stack([c.T, -s.T], axis=2)[: n1 // 2] * herm / n
    fc = fc.reshape(n1 // 2, 2 * kept)
    eye = jnp.eye(SUB, dtype=F32)
    k1 = k1[:kept]
    kk = k1[:, None, None] + n1 * jnp.arange(n2)[None, :, None]
    th = _angles(kk, jnp.arange(n2)[None, None, :], n)
    c, s = jnp.cos(th), jnp.sin(th)
    gf = jnp.concatenate([jnp.concatenate([c, s], 2), jnp.concatenate([-s, c], 2)], 1)
    ct, st = jnp.swapaxes(c, 1, 2), jnp.swapaxes(s, 1, 2)
    gi = jnp.concatenate([jnp.concatenate([ct, -st], 2), jnp.concatenate([st, ct], 2)], 1)
    return dict(n1=n1, n2=n2, kept=kept,
                fk_full=_split(jnp.kron(fa, eye)),
                fk=jnp.kron(fa[:, : n1 // 2], eye).astype(BF16),
                fki=jnp.kron(fc, eye).astype(BF16),
                gf=_split(gf), gi=gi.astype(BF16))


def _resident(shape):
    return pl.BlockSpec(shape, lambda *_: (0,) * len(shape), pipeline_mode=pl.Buffered(1))


def _spectrum_kernel(k_ref, fkh_ref, fkl_ref, gh_ref, gl_ref, o_ref, a_ref):
    rows, groups, _, tc = k_ref.shape
    n1 = gh_ref.shape[0]
    unroll = _chain_unroll(n1)

    def stage_a(g, carry):
        xh, xl = _split(k_ref[:, g].reshape(rows * SUB, tc))
        r = _dot(fkh_ref[...], xh) + _dot(fkh_ref[...], xl) + _dot(fkl_ref[...], xh)
        a_ref[:, :, g] = r.reshape(n1, 2, SUB, tc)
        return carry

    lax.fori_loop(0, groups, stage_a, 0)

    def stage_b(kk, carry):
        for u in range(unroll):
            k = kk * unroll + u
            ah, al = _split(a_ref[k].reshape(2 * groups * SUB, tc))
            o_ref[k] = _dot(gh_ref[k], ah) + _dot(gh_ref[k], al) + _dot(gl_ref[k], ah)
        return carry

    lax.fori_loop(0, n1 // unroll, stage_b, 0)


def filter_spectrum(kc, tabs, tc):
    n1, n2, kept = tabs["n1"], tabs["n2"], tabs["kept"]
    orders, _, c = kc.shape
    groups = n2 // SUB
    fkh, fkl = tabs["fk_full"]
    gh, gl = tabs["gf"]
    return pl.pallas_call(
        _spectrum_kernel,
        grid=(orders, c // tc),
        in_specs=[pl.BlockSpec((None, n1, groups, SUB, tc), lambda o, j: (o, 0, 0, 0, j)),
                  _resident(fkh.shape), _resident(fkl.shape), _resident(gh.shape), _resident(gl.shape)],
        out_specs=pl.BlockSpec((None, kept, 2 * n2, tc), lambda o, j: (o, 0, 0, j)),
        out_shape=jax.ShapeDtypeStruct((orders, kept, 2 * n2, c), F32),
        scratch_shapes=[pltpu.VMEM((kept, 2, groups, SUB, tc), F32)],
        compiler_params=_cparams("parallel", "arbitrary"),
        name="hyena_filter_spectrum",
    )(kc.reshape(orders, n1, groups, SUB, c), fkh, fkl, gh, gl)


def _hyena_conv_kernel(z_ref, x_ref, d_ref, kf_ref, fk_ref, fki_ref, gf_ref, gi_ref, o_ref, a_ref):
    rows, groups, _, tc = z_ref.shape
    n1 = gf_ref.shape[0]
    unroll = _chain_unroll(n1)
    half = groups * SUB

    def stage_a(gg, carry):
        for u in range(STAGE_AC_UNROLL):
            g = gg * STAGE_AC_UNROLL + u
            xg = z_ref[:, g].reshape(rows * SUB, tc).astype(BF16)
            a_ref[:, :, g] = _dot(fk_ref[...], xg).reshape(n1, 2, SUB, tc)
        return carry

    lax.fori_loop(0, groups // STAGE_AC_UNROLL, stage_a, 0)

    def stage_b(kk, carry):
        for u in range(unroll):
            k = kk * unroll + u
            x = _dot(gf_ref[k], a_ref[k].reshape(2 * half, tc).astype(BF16))
            kf = kf_ref[k]
            xr, xi = x[:half], x[half:]
            kr, ki = kf[:half], kf[half:]
            p = jnp.concatenate([xr * kr - xi * ki, xr * ki + xi * kr], axis=0).astype(BF16)
            a_ref[k] = _dot(gi_ref[k], p).reshape(2, groups, SUB, tc)
        return carry

    lax.fori_loop(0, n1 // unroll, stage_b, 0)

    d = d_ref[...].reshape(1, 1, tc)

    def stage_c(gg, carry):
        for u in range(STAGE_AC_UNROLL):
            g = gg * STAGE_AC_UNROLL + u
            bg = a_ref[:, :, g].reshape(n1 * 2 * SUB, tc).astype(BF16)
            y = _dot(fki_ref[...], bg).reshape(rows, SUB, tc)
            o_ref[:, g] = x_ref[:, g] * (y + z_ref[:, g] * d)
        return carry

    lax.fori_loop(0, groups // STAGE_AC_UNROLL, stage_c, 0)


def hyena_conv(z, x, d, kf, order, tabs, tc):
    n1, n2, kept = tabs["n1"], tabs["n2"], tabs["kept"]
    b, seq_len, c = z.shape
    rows, groups = n1 // 2, n2 // SUB
    view = lambda a: a.reshape(b, rows, groups, SUB, c)
    seq_spec = pl.BlockSpec((None, rows, groups, SUB, tc), lambda j, i: (i, 0, 0, 0, j))
    gf = tabs["gf"][0]
    out = pl.pallas_call(
        _hyena_conv_kernel,
        grid=(c // tc, b),
        in_specs=[seq_spec, seq_spec,
                  pl.BlockSpec((1, tc), lambda j, i: (0, j)),
                  pl.BlockSpec((None, kept, 2 * n2, tc), lambda j, i: (order, 0, 0, j),
                               pipeline_mode=pl.Buffered(1)),
                  _resident(tabs["fk"].shape), _resident(tabs["fki"].shape),
                  _resident(gf.shape), _resident(tabs["gi"].shape)],
        out_specs=seq_spec,
        out_shape=jax.ShapeDtypeStruct((b, rows, groups, SUB, c), F32),
        scratch_shapes=[pltpu.VMEM((kept, 2, groups, SUB, tc), F32)],
        compiler_params=_cparams("parallel", "arbitrary"),
        name="hyena_conv",
    )(view(z), view(x), d.reshape(1, c), kf, tabs["fk"], tabs["fki"], gf, tabs["gi"])
    return out.reshape(b, seq_len, c)


def hyena_mixer(proj, batch, seq_len, tabs, w_sc, b_sc, w1, b1, fr1, w2, b2, fr2, w3, hy_d):
    c = HY_WIDTH
    kc = hyena_filter(seq_len, w1, b1, fr1, w2, b2, fr2, w3)
    tc = 2 * LANES if seq_len >= 4096 else c
    kf = filter_spectrum(kc, tabs, tc)
    v, x1, x2 = short_conv(proj, w_sc, b_sc, seq_len)
    z = v.reshape(batch, seq_len, c)
    for o, x in enumerate((x1, x2)):
        z = hyena_conv(z, x.reshape(batch, seq_len, c), hy_d[o], kf, o, tabs, tc)
    return z.reshape(batch * seq_len, c)


def _pool_kernel(u_ref, w_ref, s_ref, o_ref, pad_ref):
    seq_len = u_ref.shape[0]
    h = POOL_HALO
    zeros = jnp.zeros((h, POOL_WIDTH), F32)
    pad_ref[0:h, :] = zeros
    pad_ref[h + seq_len:h + seq_len + h, :] = zeros
    pad_ref[h:h + seq_len, :] = u_ref[...].astype(F32)
    t = lax.broadcasted_iota(jnp.int32, (seq_len, 1), 0)
    for g, w in enumerate(POOL_WINDOWS):
        cs = slice(g * POOL_GW, (g + 1) * POOL_GW)
        tot = pad_ref[h - w // 2:h - w // 2 + seq_len, cs]
        for d in range(1 - w // 2, w - w // 2):
            tot = tot + pad_ref[h + d:h + d + seq_len, cs]
        cnt = (jnp.minimum(t - w // 2 + w, seq_len) - jnp.maximum(t - w // 2, 0)).astype(F32)
        pooled = tot / cnt - pad_ref[h:h + seq_len, cs]
        o_ref[:, cs] = (_dot(pooled.astype(BF16), w_ref[g]) * s_ref[:, cs]).astype(o_ref.dtype)


def pool_mixer(proj, w_pool, pool_scale, batch, seq_len):
    t = proj.shape[0]
    cb = (IN_COLS - POOL_WIDTH) // POOL_WIDTH
    return pl.pallas_call(
        _pool_kernel,
        grid=(batch,),
        in_specs=[pl.BlockSpec((seq_len, POOL_WIDTH), lambda b: (b, cb)),
                  pl.BlockSpec((len(POOL_WINDOWS), POOL_GW, POOL_GW), lambda b: (0, 0, 0)),
                  pl.BlockSpec((1, POOL_WIDTH), lambda b: (0, 0))],
        out_specs=pl.BlockSpec((seq_len, POOL_WIDTH), lambda b: (b, 0)),
        out_shape=jax.ShapeDtypeStruct((t, POOL_WIDTH), BF16),
        scratch_shapes=[pltpu.VMEM((seq_len + 2 * POOL_HALO, POOL_WIDTH), F32)],
        compiler_params=_cparams("parallel"),
        name="pool_mixer",
    )(proj, w_pool.astype(BF16), pool_scale.reshape(1, POOL_WIDTH))


def _outproj_kernel(att_ref, hy_ref, po_ref, x_ref, wa_ref, wh_ref, wp_ref, g1_ref, g_ref, sc_ref, sh_ref,
                    x1_ref, h2_ref):
    mix = (_dot(att_ref[...], wa_ref[...]) + _dot(hy_ref[...].astype(BF16), wh_ref[...])
           + _dot(po_ref[...], wp_ref[...]))
    x1 = x_ref[...] + g1_ref[...] * mix
    x1_ref[...] = x1
    y = x1 * lax.rsqrt(jnp.mean(x1 * x1, axis=-1, keepdims=True) + EPS) * g_ref[...]
    h2_ref[...] = (y * (1.0 + sc_ref[...]) + sh_ref[...]).astype(BF16)


def out_proj(att, hy, po, x, w_out, norm_g, mod, row_of_token, tm=512):
    t, d = x.shape
    a0, a1 = ATT_WIDTH, ATT_WIDTH + HY_WIDTH
    const = lambda shape: pl.BlockSpec(shape, lambda i: (0, 0))
    return pl.pallas_call(
        _outproj_kernel,
        grid=(t // tm,),
        in_specs=[
            pl.BlockSpec((tm, ATT_WIDTH), lambda i: (i, 0)),
            pl.BlockSpec((tm, HY_WIDTH), lambda i: (i, 0)),
            pl.BlockSpec((tm, POOL_WIDTH), lambda i: (i, 0)),
            pl.BlockSpec((tm, d), lambda i: (i, 0)),
            const((ATT_WIDTH, d)),
            pl.BlockSpec((HY_WIDTH, d), lambda i: (a0 // HY_WIDTH, 0)),
            pl.BlockSpec((POOL_WIDTH, d), lambda i: (a1 // POOL_WIDTH, 0)),
            _mod_spec(2, row_of_token, tm),
            const((1, d)),
            _mod_spec(4, row_of_token, tm),
            _mod_spec(3, row_of_token, tm),
        ],
        out_specs=[pl.BlockSpec((tm, d), lambda i: (i, 0))] * 2,
        out_shape=[jax.ShapeDtypeStruct((t, d), F32), jax.ShapeDtypeStruct((t, d), BF16)],
        compiler_params=_cparams("parallel"),
        name="out_proj",
    )(att, hy, po, x, w_out, w_out, w_out, mod, norm_g.reshape(1, d), mod, mod)


def _ffn_kernel(h_ref, x_ref, wg_ref, wu_ref, wd_ref, g2_ref, o_ref, acc_ref):
    j = pl.program_id(1)

    @pl.when(j == 0)
    def _():
        acc_ref[...] = jnp.zeros_like(acc_ref)

    h = h_ref[...]
    gate = _dot(h, wg_ref[...])
    up = _dot(h, wu_ref[...])
    act = (gate * (1.0 / (1.0 + jnp.exp(-gate))) * up).astype(BF16)
    acc_ref[...] += _dot(act, wd_ref[...])

    @pl.when(j == pl.num_programs(1) - 1)
    def _():
        o_ref[...] = x_ref[...] + g2_ref[...] * acc_ref[...]


def ffn(h2, x1, w_gate, w_up, w_down, mod, row_of_token, tm=512, th=512):
    t, d = x1.shape
    hidden = w_gate.shape[1]
    tm = min(tm, t)
    return pl.pallas_call(
        _ffn_kernel,
        grid=(t // tm, hidden // th),
        in_specs=[
            pl.BlockSpec((tm, d), lambda i, j: (i, 0)),
            pl.BlockSpec((tm, d), lambda i, j: (i, 0)),
            pl.BlockSpec((d, th), lambda i, j: (0, j)),
            pl.BlockSpec((d, th), lambda i, j: (0, j)),
            pl.BlockSpec((th, d), lambda i, j: (j, 0)),
            _mod_spec(5, row_of_token, tm),
        ],
        out_specs=pl.BlockSpec((tm, d), lambda i, j: (i, 0)),
        out_shape=jax.ShapeDtypeStruct((t, d), F32),
        scratch_shapes=[pltpu.VMEM((tm, d), F32)],
        compiler_params=_cparams("parallel", "arbitrary"),
        name="ffn",
    )(h2, x1, w_gate, w_up, w_down, mod)


def trunk_layer(x, batch, seq_len, mod, row_of_token, rope_tabs, cache, layer, lam_init, tabs, p, emit_state):
    proj = in_proj(x, p["norm1"], mod, row_of_token, p["w_in"], F32 if emit_state else BF16)
    prep = qk_prep(proj, p["q_norm"], p["k_norm"], rope_tabs, seq_len, emit_state)
    cache_k, cache_v = cache if cache is not None else (None, None)
    att = diff_attn(prep[0], prep[1], proj, cache_k, cache_v, layer, p["w_lambda"], p["subln"], lam_init,
                    batch, seq_len)
    hy = hyena_mixer(proj, batch, seq_len, tabs, p["w_sc"], p["b_sc"], p["hy_w1"], p["hy_b1"], p["hy_fr1"],
                     p["hy_w2"], p["hy_b2"], p["hy_fr2"], p["hy_w3"], p["hy_d"])
    po = pool_mixer(proj, p["w_pool"], p["pool_scale"], batch, seq_len)
    x1, h2 = out_proj(att, hy, po, x, p["w_out"], p["norm2"], mod, row_of_token)
    y = ffn(h2, x1, p["w_gate"], p["w_up"], p["w_down"], mod, row_of_token)
    if emit_state:
        return y, prep[2], proj[:, 2 * ATT_WIDTH:3 * ATT_WIDTH]
    return y, None, None


def kernel(x_prompt, x_sample, cache_k, cache_v, c, c_ctx, w_ada, b_ada, norm1, norm2, w_in, q_norm, k_norm, w_lambda, subln, w_sc, b_sc, hy_w1, hy_b1, hy_fr1, hy_w2, hy_b2, hy_fr2, hy_w3, hy_d, w_pool, pool_scale, w_out, w_gate, w_up, w_down):
    batch, seq, d = x_prompt.shape
    dec_batch, dec_seq, _ = x_sample.shape
    depth = w_ada.shape[0]
    past = cache_k.shape[2]

    cond = jnp.concatenate([c_ctx[None, :], c, jnp.zeros((MOD_ROWS - 1 - dec_batch, d), F32)], axis=0)
    mod_all = ada_mod(cond, w_ada, b_ada)
    rope_tabs = rope_tables(dec_seq)
    tabs_ctx, tabs_lat = dft_tables(seq), dft_tables(dec_seq)
    ck = cache_k.reshape(dec_batch, depth, past, ATT_WIDTH)
    cv = cache_v.reshape(dec_batch, depth, past, ATT_WIDTH)

    y_p = x_prompt.reshape(batch * seq, d)
    y_s = x_sample.reshape(dec_batch * dec_seq, d)
    ks, vs = [], []
    for l in range(depth):
        p = dict(norm1=norm1[l], norm2=norm2[l], w_in=w_in[l].astype(BF16), q_norm=q_norm[l], k_norm=k_norm[l],
                 w_lambda=w_lambda[l], subln=subln[l], w_sc=w_sc[l], b_sc=b_sc[l], hy_w1=hy_w1[l],
                 hy_b1=hy_b1[l], hy_fr1=hy_fr1[l], hy_w2=hy_w2[l], hy_b2=hy_b2[l], hy_fr2=hy_fr2[l],
                 hy_w3=hy_w3[l], hy_d=hy_d[l], w_pool=w_pool[l], pool_scale=pool_scale[l],
                 w_out=w_out[l].astype(BF16), w_gate=w_gate[l].astype(BF16), w_up=w_up[l].astype(BF16),
                 w_down=w_down[l].astype(BF16))
        lam_init = 0.8 - 0.6 * math.exp(-0.3 * l)
        mod = mod_all[l].reshape(MOD_ROWS * 6, 1, d)
        y_p, k_l, v_l = trunk_layer(y_p, batch, seq, mod, lambda tok: 0, None, None, l, lam_init,
                                    tabs_ctx, p, True)
        ks.append(k_l.reshape(batch, seq, N_HEADS, 2, ATT_HD))
        vs.append(v_l.reshape(batch, seq, N_HEADS, ATT_VD))
        y_s, _, _ = trunk_layer(y_s, dec_batch, dec_seq, mod, lambda tok: 1 + tok // dec_seq, rope_tabs,
                                (ck, cv), l, lam_init, tabs_lat, p, False)
    return (y_p.reshape(batch, seq, d), y_s.reshape(dec_batch, dec_seq, d),
            jnp.stack(ks, axis=1), jnp.stack(vs, axis=1))
```

```python
import functools
import math

import jax
import jax.numpy as jnp
import numpy as np
from jax import lax
from jax.experimental import pallas as pl
from jax.experimental.pallas import tpu as pltpu

F32 = jnp.float32
BF16 = jnp.bfloat16

D_MODEL = 2048
DEPTH = 2
GRID_W = 64
ATT_WIDTH = 1024
HY_WIDTH = 512
POOL_WIDTH = 512
ATT_HD = 64
ATT_VD = 128
N_HEADS = 8
ROPE_THETA = 10000.0
HY_ORDER = 2
HY_BANDS = 16
HY_EMB = 1 + 2 * HY_BANDS
HY_HIDDEN = 64
HY_DECAY_TARGET = 1e-2
HY_FAST_DECAY = 0.3
HY_SLOW_DECAY = 1.5
HY_DECAY_SHIFT = 0.05
HY_MIN_DECAY = math.log(HY_DECAY_TARGET) / HY_SLOW_DECAY
HY_MAX_DECAY = math.log(HY_DECAY_TARGET) / HY_FAST_DECAY
POOL_WINDOWS = (2, 4, 8, 16)
POOL_GW = 128
POOL_HALO = 16
FFN_HIDDEN = 5632
IN_COLS = 5120
EPS = 1e-6
Q_SCALE = ATT_HD ** -0.5 * math.log2(math.e)
MOD_ROWS = 16
EMB_PAD = 128
LANES = 128
MXU_WIDTH = 256
VMEM_LIMIT = 56 * 1024 * 1024


def _cparams(*sem):
    return pltpu.CompilerParams(dimension_semantics=sem, vmem_limit_bytes=VMEM_LIMIT)


def _split(x):
    hi = x.astype(BF16)
    lo = (x - hi.astype(F32)).astype(BF16)
    return hi, lo


def _dot(a, b):
    return jnp.dot(a, b, preferred_element_type=F32)


def _dot3(a, b):
    ah, al = _split(a)
    bh, bl = _split(b)
    return _dot(ah, bh) + _dot(al, bh) + _dot(ah, bl)


def _ada_kernel(c_ref, w_ref, b_ref, o_ref):
    c = c_ref[...]
    a = c * (1.0 / (1.0 + jnp.exp(-c)))
    o_ref[...] = _dot3(a, w_ref[...]) + b_ref[...]


def ada_mod(cond, w_ada, b_ada, tn=1024):
    depth, d, n = w_ada.shape
    rows = cond.shape[0]
    return pl.pallas_call(
        _ada_kernel,
        grid=(depth, n // tn),
        in_specs=[
            pl.BlockSpec((rows, d), lambda l, j: (0, 0)),
            pl.BlockSpec((None, d, tn), lambda l, j: (l, 0, j)),
            pl.BlockSpec((None, 1, tn), lambda l, j: (l, 0, j)),
        ],
        out_specs=pl.BlockSpec((None, rows, tn), lambda l, j: (l, 0, j)),
        out_shape=jax.ShapeDtypeStruct((depth, rows, n), F32),
        compiler_params=_cparams("parallel", "parallel"),
        name="ada_mod",
    )(cond, w_ada, b_ada.reshape(depth, 1, n))


def _mod_spec(part, row_of_token, tm):
    return pl.BlockSpec((None, 1, D_MODEL), lambda i, *_: (row_of_token(i * tm) * 6 + part, 0, 0))


def _inproj_kernel(x_ref, g_ref, sc_ref, sh_ref, w_ref, o_ref, h_ref):
    @pl.when(pl.program_id(1) == 0)
    def _():
        x = x_ref[...]
        y = x * lax.rsqrt(jnp.mean(x * x, axis=-1, keepdims=True) + EPS) * g_ref[...]
        h_ref[...] = (y * (1.0 + sc_ref[...]) + sh_ref[...]).astype(BF16)

    o_ref[...] = _dot(h_ref[...], w_ref[...]).astype(o_ref.dtype)


def in_proj(x, norm_g, mod, row_of_token, w_in, out_dtype, tm=1024, tn=1280):
    t, d = x.shape
    n = w_in.shape[1]
    tm = min(tm, t)
    return pl.pallas_call(
        _inproj_kernel,
        grid=(t // tm, n // tn),
        in_specs=[
            pl.BlockSpec((tm, d), lambda i, j: (i, 0)),
            pl.BlockSpec((1, d), lambda i, j: (0, 0)),
            _mod_spec(1, row_of_token, tm),
            _mod_spec(0, row_of_token, tm),
            pl.BlockSpec((d, tn), lambda i, j: (0, j)),
        ],
        out_specs=pl.BlockSpec((tm, tn), lambda i, j: (i, j)),
        out_shape=jax.ShapeDtypeStruct((t, n), out_dtype),
        scratch_shapes=[pltpu.VMEM((tm, d), BF16)],
        compiler_params=_cparams("parallel", "arbitrary"),
        name="in_proj",
    )(x, norm_g.reshape(1, d), mod, mod, w_in)


def _group_rms(y, group_ones):
    hi, lo = _split(y * y)
    ss = _dot(hi, group_ones) + _dot(lo, group_ones)
    return y * lax.rsqrt(ss * (1.0 / ATT_HD) + EPS)


def _rope(y, cos, sin, lane):
    swap = jnp.where((lane & (ATT_HD // 2)) == 0,
                     pltpu.roll(y, LANES - ATT_HD // 2, 1), pltpu.roll(y, ATT_HD // 2, 1))
    return y * cos + swap * sin


def _qkprep_kernel(*refs, rope, emit_state):
    if rope:
        q_ref, k_ref, gq_ref, gk_ref, cos_ref, sin_ref = refs[:6]
        outs = refs[6:]
    else:
        q_ref, k_ref, gq_ref, gk_ref = refs[:4]
        outs = refs[4:]
    qo_ref, ko_ref = outs[0], outs[1]
    tm = q_ref.shape[0]
    lane = lax.broadcasted_iota(jnp.int32, (tm, LANES), 1)
    gr = lax.broadcasted_iota(jnp.int32, (LANES, LANES), 0) // ATT_HD
    gc = lax.broadcasted_iota(jnp.int32, (LANES, LANES), 1) // ATT_HD
    group_ones = jnp.where(gr == gc, 1.0, 0.0).astype(BF16)
    for h in range(N_HEADS):
        cs = slice(h * LANES, (h + 1) * LANES)
        q = _group_rms(q_ref[:, cs].astype(F32), group_ones) * gq_ref[...]
        k = _group_rms(k_ref[:, cs].astype(F32), group_ones) * gk_ref[...]
        if emit_state:
            outs[2][:, cs] = k
        if rope:
            q = _rope(q, cos_ref[...], sin_ref[...], lane)
            k = _rope(k, cos_ref[...], sin_ref[...], lane)
        qo_ref[:, cs] = (q * Q_SCALE).astype(BF16)
        ko_ref[:, cs] = k.astype(BF16)


def qk_prep(proj, q_gain, k_gain, rope_tabs, seq_len, emit_state, tm=512):
    t = proj.shape[0]
    tm = min(tm, seq_len)
    gq = jnp.tile(q_gain.reshape(1, ATT_HD), (1, 2))
    gk = jnp.tile(k_gain.reshape(1, ATT_HD), (1, 2))
    in_specs = [
        pl.BlockSpec((tm, ATT_WIDTH), lambda i: (i, 0)),
        pl.BlockSpec((tm, ATT_WIDTH), lambda i: (i, 1)),
        pl.BlockSpec((1, LANES), lambda i: (0, 0)),
        pl.BlockSpec((1, LANES), lambda i: (0, 0)),
    ]
    args = [proj, proj, gq, gk]
    if rope_tabs is not None:
        nblk = seq_len // tm
        in_specs += [pl.BlockSpec((tm, LANES), lambda i: (i % nblk, 0))] * 2
        args += list(rope_tabs)
    out_shape = [jax.ShapeDtypeStruct((t, ATT_WIDTH), BF16)] * 2
    out_specs = [pl.BlockSpec((tm, ATT_WIDTH), lambda i: (i, 0))] * 2
    if emit_state:
        out_shape.append(jax.ShapeDtypeStruct((t, ATT_WIDTH), F32))
        out_specs.append(pl.BlockSpec((tm, ATT_WIDTH), lambda i: (i, 0)))
    return pl.pallas_call(
        functools.partial(_qkprep_kernel, rope=rope_tabs is not None, emit_state=emit_state),
        grid=(t // tm,),
        in_specs=in_specs,
        out_specs=out_specs,
        out_shape=out_shape,
        compiler_params=_cparams("parallel"),
        name="qk_prep",
    )(*args)


def rope_tables(n_tokens):
    pos = np.arange(n_tokens)
    rows = (pos // GRID_W).astype(np.float32)
    cols = (pos % GRID_W).astype(np.float32)
    quarter = ATT_HD // 4
    inv = (ROPE_THETA ** (-np.arange(quarter, dtype=np.float32) / quarter)).astype(np.float32)
    ang = np.concatenate([rows[:, None] * inv, cols[:, None] * inv], -1).astype(np.float64)
    cos, sin = np.cos(ang).astype(np.float32), np.sin(ang).astype(np.float32)
    return np.tile(cos, (1, 4)), np.concatenate([-sin, sin, -sin, sin], -1)


def _attn_kernel(*refs, chunks, has_ctx, lam_init):
    if has_ctx:
        q_ref, k_ref, v_ref, ck_ref, cv_ref, lw_ref, sub_ref, o_ref, s_ref, ka_ref, va_ref, m_ref, acc_ref = refs
    else:
        q_ref, k_ref, v_ref, lw_ref, sub_ref, o_ref, s_ref, ka_ref, va_ref, m_ref, acc_ref = refs
    tq = q_ref.shape[0]
    seq = k_ref.shape[0]

    @pl.when(pl.program_id(2) == 0)
    def _():
        ka_ref[0:seq, :] = k_ref[...]
        va_ref[0:seq, :] = v_ref[...].astype(BF16)
        if has_ctx:
            ka_ref[seq:, :] = ck_ref[...].astype(BF16)
            va_ref[seq:, :] = cv_ref[...].astype(BF16)

    q = q_ref[...]
    lane = lax.broadcasted_iota(jnp.int32, (tq, LANES), 1)
    zero = jnp.zeros_like(q)
    qz = jnp.concatenate([jnp.where(lane < ATT_HD, q, zero), jnp.where(lane < ATT_HD, zero, q)], axis=0)

    def scores(kc):
        return lax.dot_general(qz, kc, (((1,), (1,)), ((), ())), preferred_element_type=F32)

    def fold_max(m, s):
        for t in range(s.shape[1] // LANES):
            m = jnp.maximum(m, s[:, t * LANES:(t + 1) * LANES])
        return m

    def weighted(s, vc, mb):
        p = jnp.exp2(s - jnp.concatenate([mb] * (s.shape[1] // LANES), axis=1)).astype(BF16)
        return _dot(p, jnp.concatenate([vc, jnp.ones_like(vc)], axis=1))

    one = jnp.minimum(pl.program_id(2), 0) + 1

    def region(body):
        lax.fori_loop(0, one, lambda _, carry: body() or carry, 0)

    for n, (lo, hi) in enumerate(chunks):
        def score_chunk(n=n, lo=lo, hi=hi):
            s = scores(ka_ref[lo:hi, :])
            s_ref[:, lo:hi] = s
            m0 = jnp.full((2 * tq, LANES), -jnp.inf, F32) if n == 0 else m_ref[...]
            m = fold_max(m0, s)
            if n == len(chunks) - 1:
                m = jnp.broadcast_to(jnp.max(m, axis=-1, keepdims=True), (2 * tq, LANES))
            m_ref[...] = m

        region(score_chunk)

    for n, (lo, hi) in enumerate(chunks):
        def weigh_chunk(n=n, lo=lo, hi=hi):
            w = weighted(s_ref[:, lo:hi], va_ref[lo:hi, :], m_ref[...])
            acc_ref[...] = w if n == 0 else acc_ref[...] + w

        region(weigh_chunk)

    acc, l = acc_ref[:, :ATT_VD], acc_ref[:, ATT_VD:]

    lw = lw_ref[...]
    lam = (jnp.exp(jnp.sum(lw[0:1] * lw[1:2], axis=-1, keepdims=True))
           - jnp.exp(jnp.sum(lw[2:3] * lw[3:4], axis=-1, keepdims=True)) + lam_init)
    o = acc[:tq] / l[:tq] - lam * (acc[tq:] / l[tq:])
    o = o * lax.rsqrt(jnp.mean(o * o, axis=-1, keepdims=True) + EPS) * sub_ref[...]
    o_ref[...] = (o * (1.0 - lam_init)).astype(o_ref.dtype)


def diff_attn(qn, kn, proj, cache_k, cache_v, layer, w_lambda, subln, lam_init, batch, seq_len,
              tq=512, max_tiles=9):
    t = qn.shape[0]
    tq = min(tq, seq_len)
    nq = seq_len // tq
    has_ctx = cache_k is not None
    total = seq_len + (cache_k.shape[2] if has_ctx else 0)
    tiles = total // MXU_WIDTH
    assert tiles * MXU_WIDTH == total
    n_chunks = -(-tiles // max_tiles)
    bounds = [MXU_WIDTH * ((tiles * c + n_chunks - 1) // n_chunks) for c in range(n_chunks + 1)]
    chunks = tuple(zip(bounds[:-1], bounds[1:]))
    v_col0 = 2 * ATT_WIDTH // LANES
    in_specs = [
        pl.BlockSpec((tq, LANES), lambda b, h, i: (b * nq + i, h)),
        pl.BlockSpec((seq_len, LANES), lambda b, h, i: (b, h)),
        pl.BlockSpec((seq_len, LANES), lambda b, h, i: (b, v_col0 + h)),
    ]
    args = [qn, kn, proj]
    if has_ctx:
        past = cache_k.shape[2]
        in_specs += [pl.BlockSpec((None, None, past, LANES), lambda b, h, i: (b, layer, 0, h))] * 2
        args += [cache_k, cache_v]
    in_specs += [pl.BlockSpec((4, ATT_HD), lambda b, h, i: (0, 0)),
                 pl.BlockSpec((1, ATT_VD), lambda b, h, i: (0, 0))]
    args += [w_lambda, subln.reshape(1, ATT_VD)]
    return pl.pallas_call(
        functools.partial(_attn_kernel, chunks=chunks, has_ctx=has_ctx, lam_init=lam_init),
        grid=(batch, N_HEADS, nq),
        in_specs=in_specs,
        out_specs=pl.BlockSpec((tq, LANES), lambda b, h, i: (b * nq + i, h)),
        out_shape=jax.ShapeDtypeStruct((t, ATT_WIDTH), BF16),
        scratch_shapes=[pltpu.VMEM((2 * tq, total), F32),
                        pltpu.VMEM((total, LANES), BF16), pltpu.VMEM((total, LANES), BF16),
                        pltpu.VMEM((2 * tq, LANES), F32), pltpu.VMEM((2 * tq, 2 * ATT_VD), F32)],
        compiler_params=_cparams("parallel", "parallel", "arbitrary"),
        name="diff_attn",
    )(*args)


def _shortconv_kernel(u_ref, prev_ref, next_ref, w_ref, b_ref, v_ref, x1_ref, x2_ref, *, tiles_per_seq):
    i = pl.program_id(0)
    tl = u_ref.shape[0]
    u = u_ref[...].astype(F32)
    first = (i % tiles_per_seq) == 0
    last = (i % tiles_per_seq) == tiles_per_seq - 1
    sub = prev_ref.shape[0]
    prev_row = jnp.where(first, 0.0, prev_ref[sub - 1:sub, :].astype(F32))
    next_row = jnp.where(last, 0.0, next_ref[0:1, :].astype(F32))
    row = lax.broadcasted_iota(jnp.int32, u.shape, 0)
    below = jnp.where(row == 0, prev_row, pltpu.roll(u, 1, 0))
    above = jnp.where(row == tl - 1, next_row, pltpu.roll(u, tl - 1, 0))
    w = w_ref[...]
    y = below * w[0:1] + u * w[1:2] + above * w[2:3] + b_ref[...]
    v_ref[...] = y[:, :HY_WIDTH]
    x1_ref[...] = y[:, HY_WIDTH:2 * HY_WIDTH]
    x2_ref[...] = y[:, 2 * HY_WIDTH:]


def short_conv(proj, w_sc, b_sc, seq_len, tl=512):
    t = proj.shape[0]
    width = 3 * HY_WIDTH
    cb = 3 * ATT_WIDTH // width
    tl = min(tl, seq_len)
    sub = 16
    nsub = tl // sub
    last_sub = t // sub - 1
    part = jax.ShapeDtypeStruct((t, HY_WIDTH), F32)
    return pl.pallas_call(
        functools.partial(_shortconv_kernel, tiles_per_seq=seq_len // tl),
        grid=(t // tl,),
        in_specs=[
            pl.BlockSpec((tl, width), lambda i: (i, cb)),
            pl.BlockSpec((sub, width), lambda i: (jnp.maximum(i * nsub - 1, 0), cb)),
            pl.BlockSpec((sub, width), lambda i: (jnp.minimum((i + 1) * nsub, last_sub), cb)),
            pl.BlockSpec((3, width), lambda i: (0, 0)),
            pl.BlockSpec((1, width), lambda i: (0, 0)),
        ],
        out_specs=[pl.BlockSpec((tl, HY_WIDTH), lambda i: (i, 0))] * 3,
        out_shape=[part] * 3,
        compiler_params=_cparams("parallel"),
        name="short_conv",
    )(proj, proj, proj, w_sc, b_sc.reshape(1, width))


def _filter_kernel(z_ref, w1_ref, b1_ref, f1_ref, w2_ref, b2_ref, f2_ref, w3f_ref, w3b_ref, dl_ref,
                   o_ref, h_ref, *, seq_len):
    @pl.when(pl.program_id(0) == 0)
    def _():
        h = jnp.sin(f1_ref[...] * (_dot3(z_ref[...], w1_ref[...]) + b1_ref[...]))
        h_ref[...] = jnp.sin(f2_ref[...] * (_dot3(h, w2_ref[...]) + b2_ref[...]))

    n = 2 * seq_len
    t = z_ref[:, 0:1]
    window = jnp.exp(-t * dl_ref[...]) + HY_DECAY_SHIFT
    fwd = _dot3(h_ref[0:seq_len, :], w3f_ref[...])
    bwd = _dot3(h_ref[seq_len:n, :], w3b_ref[...])
    row = lax.broadcasted_iota(jnp.int32, (seq_len, 1), 0)
    bwd = jnp.where(row == 0, 0.0, bwd)
    fwd = fwd * window[0:seq_len]
    bwd = bwd * window[seq_len:n]
    norm = jnp.sum(jnp.abs(fwd), axis=0, keepdims=True) + jnp.sum(jnp.abs(bwd), axis=0, keepdims=True)
    o_ref[0:seq_len, :] = fwd / norm
    o_ref[seq_len:n, :] = bwd / norm


def hyena_embedding(seq_len):
    t = np.linspace(0.0, 1.0, seq_len, dtype=np.float32)[:, None]
    wpos = (np.float32(2.0 * math.pi) * np.arange(seq_len, dtype=np.float32) / np.float32(seq_len))[:, None]
    bands = np.linspace(1e-4, HY_BANDS - 1, HY_BANDS, dtype=np.float32)[None, :]
    arg = (bands * wpos).astype(np.float64)
    z = np.concatenate([t, np.cos(arg).astype(np.float32), np.sin(arg).astype(np.float32)], -1)
    z = np.concatenate([z, z[:1], z[1:][::-1]], axis=0)
    return np.pad(z, ((0, 0), (0, EMB_PAD - HY_EMB)))


def hyena_filter(seq_len, w1, b1, fr1, w2, b2, fr2, w3, tn=128):
    n = 2 * seq_len
    z = hyena_embedding(seq_len)
    w1p = jnp.pad(w1, ((0, EMB_PAD - HY_EMB), (0, 0)))
    deltas = jnp.abs(jnp.linspace(HY_MIN_DECAY, HY_MAX_DECAY, HY_WIDTH, dtype=F32)).reshape(1, HY_WIDTH)
    cpo = HY_WIDTH // tn
    nblk = HY_ORDER * cpo
    row = lambda a: a.reshape(1, -1)
    full = lambda shape: pl.BlockSpec(shape, lambda j: (0,) * len(shape))
    return pl.pallas_call(
        functools.partial(_filter_kernel, seq_len=seq_len),
        grid=(nblk,),
        in_specs=[
            full((n, EMB_PAD)), full((EMB_PAD, HY_HIDDEN)), full((1, HY_HIDDEN)), full((1, HY_HIDDEN)),
            full((HY_HIDDEN, HY_HIDDEN)), full((1, HY_HIDDEN)), full((1, HY_HIDDEN)),
            pl.BlockSpec((HY_HIDDEN, tn), lambda j: (0, j)),
            pl.BlockSpec((HY_HIDDEN, tn), lambda j: (0, nblk + j)),
            pl.BlockSpec((1, tn), lambda j: (0, j % cpo)),
        ],
        out_specs=pl.BlockSpec((None, n, tn), lambda j: (j // cpo, 0, j % cpo)),
        out_shape=jax.ShapeDtypeStruct((HY_ORDER, n, HY_WIDTH), F32),
        scratch_shapes=[pltpu.VMEM((n, HY_HIDDEN), F32)],
        compiler_params=_cparams("arbitrary"),
        name="hyena_filter",
    )(z, w1p, row(b1), row(fr1), w2, row(b2), row(fr2), w3, w3, deltas)


SUB = 8
STAGE_B_MAX_UNROLL = 11
STAGE_AC_UNROLL = 2


def _chain_unroll(trips):
    return max(u for u in range(1, STAGE_B_MAX_UNROLL + 1) if trips % u == 0)


def dft_split(seq_len):
    return (64, 128) if seq_len >= 4096 else (16, 2 * seq_len // 16)


def _angles(a, b, period):
    return (2.0 * math.pi / period) * ((a * b) % period).astype(np.float64)


def _split_const(x):
    hi = x.astype(BF16)
    return hi, (x - hi.astype(np.float32)).astype(BF16)


def dft_tables(seq_len):
    n1, n2 = dft_split(seq_len)
    n = n1 * n2
    kept = n1 // 2 + 1
    k1 = np.arange(n1)
    th = _angles(k1[:kept, None], k1[None, :], n1)
    c, s = np.cos(th), np.sin(th)
    fa = np.stack([c, -s], axis=1).reshape(2 * kept, n1)
    herm = np.where((k1[:kept] == 0) | (k1[:kept] == n1 // 2), 1.0, 2.0)[None, :, None]
    fc = np.stack([c.T, -s.T], axis=2)[: n1 // 2] * herm / n
    fc = fc.reshape(n1 // 2, 2 * kept)
    eye = np.eye(SUB)
    k1 = k1[:kept]
    kk = k1[:, None, None] + n1 * np.arange(n2)[None, :, None]
    th = _angles(kk, np.arange(n2)[None, None, :], n)
    c, s = np.cos(th), np.sin(th)
    gf = np.concatenate([np.concatenate([c, s], 2), np.concatenate([-s, c], 2)], 1)
    ct, st = np.swapaxes(c, 1, 2), np.swapaxes(s, 1, 2)
    gi = np.concatenate([np.concatenate([ct, -st], 2), np.concatenate([st, ct], 2)], 1)
    f32 = lambda a: a.astype(np.float32)
    return dict(n1=n1, n2=n2, kept=kept,
                fk_full=_split_const(f32(np.kron(fa, eye))),
                fk=f32(np.kron(fa[:, : n1 // 2], eye)).astype(BF16),
                fki=f32(np.kron(fc, eye)).astype(BF16),
                gf=_split_const(f32(gf)), gi=f32(gi).astype(BF16))


def _resident(shape):
    return pl.BlockSpec(shape, lambda *_: (0,) * len(shape), pipeline_mode=pl.Buffered(1))


def _spectrum_kernel(k_ref, fkh_ref, fkl_ref, gh_ref, gl_ref, o_ref, a_ref):
    rows, groups, _, tc = k_ref.shape
    n1 = gh_ref.shape[0]
    unroll = _chain_unroll(n1)

    def stage_a(g, carry):
        xh, xl = _split(k_ref[:, g].reshape(rows * SUB, tc))
        r = _dot(fkh_ref[...], xh) + _dot(fkh_ref[...], xl) + _dot(fkl_ref[...], xh)
        a_ref[:, :, g] = r.reshape(n1, 2, SUB, tc)
        return carry

    lax.fori_loop(0, groups, stage_a, 0)

    def stage_b(kk, carry):
        for u in range(unroll):
            k = kk * unroll + u
            ah, al = _split(a_ref[k].reshape(2 * groups * SUB, tc))
            o_ref[k] = _dot(gh_ref[k], ah) + _dot(gh_ref[k], al) + _dot(gl_ref[k], ah)
        return carry

    lax.fori_loop(0, n1 // unroll, stage_b, 0)


def filter_spectrum(kc, tabs, tc):
    n1, n2, kept = tabs["n1"], tabs["n2"], tabs["kept"]
    orders, _, c = kc.shape
    groups = n2 // SUB
    fkh, fkl = tabs["fk_full"]
    gh, gl = tabs["gf"]
    return pl.pallas_call(
        _spectrum_kernel,
        grid=(orders, c // tc),
        in_specs=[pl.BlockSpec((None, n1, groups, SUB, tc), lambda o, j: (o, 0, 0, 0, j)),
                  _resident(fkh.shape), _resident(fkl.shape), _resident(gh.shape), _resident(gl.shape)],
        out_specs=pl.BlockSpec((None, kept, 2 * n2, tc), lambda o, j: (o, 0, 0, j)),
        out_shape=jax.ShapeDtypeStruct((orders, kept, 2 * n2, c), F32),
        scratch_shapes=[pltpu.VMEM((kept, 2, groups, SUB, tc), F32)],
        compiler_params=_cparams("parallel", "arbitrary"),
        name="hyena_filter_spectrum",
    )(kc.reshape(orders, n1, groups, SUB, c), fkh, fkl, gh, gl)


def _hyena_conv_kernel(z_ref, x_ref, d_ref, kf_ref, fk_ref, fki_ref, gf_ref, gi_ref, o_ref, a_ref):
    rows, groups, _, tc = z_ref.shape
    n1 = gf_ref.shape[0]
    unroll = _chain_unroll(n1)
    half = groups * SUB

    def stage_a(gg, carry):
        for u in range(STAGE_AC_UNROLL):
            g = gg * STAGE_AC_UNROLL + u
            xg = z_ref[:, g].reshape(rows * SUB, tc).astype(BF16)
            a_ref[:, :, g] = _dot(fk_ref[...], xg).reshape(n1, 2, SUB, tc)
        return carry

    lax.fori_loop(0, groups // STAGE_AC_UNROLL, stage_a, 0)

    def stage_b(kk, carry):
        for u in range(unroll):
            k = kk * unroll + u
            x = _dot(gf_ref[k], a_ref[k].reshape(2 * half, tc).astype(BF16))
            kf = kf_ref[k]
            xr, xi = x[:half], x[half:]
            kr, ki = kf[:half], kf[half:]
            p = jnp.concatenate([xr * kr - xi * ki, xr * ki + xi * kr], axis=0).astype(BF16)
            a_ref[k] = _dot(gi_ref[k], p).reshape(2, groups, SUB, tc)
        return carry

    lax.fori_loop(0, n1 // unroll, stage_b, 0)

    d = d_ref[...].reshape(1, 1, tc)

    def stage_c(gg, carry):
        for u in range(STAGE_AC_UNROLL):
            g = gg * STAGE_AC_UNROLL + u
            bg = a_ref[:, :, g].reshape(n1 * 2 * SUB, tc).astype(BF16)
            y = _dot(fki_ref[...], bg).reshape(rows, SUB, tc)
            o_ref[:, g] = x_ref[:, g] * (y + z_ref[:, g] * d)
        return carry

    lax.fori_loop(0, groups // STAGE_AC_UNROLL, stage_c, 0)


def hyena_conv(z, x, d, kf, order, tabs, tc):
    n1, n2, kept = tabs["n1"], tabs["n2"], tabs["kept"]
    b, seq_len, c = z.shape
    rows, groups = n1 // 2, n2 // SUB
    view = lambda a: a.reshape(b, rows, groups, SUB, c)
    seq_spec = pl.BlockSpec((None, rows, groups, SUB, tc), lambda j, i: (i, 0, 0, 0, j))
    gf = tabs["gf"][0]
    out = pl.pallas_call(
        _hyena_conv_kernel,
        grid=(c // tc, b),
        in_specs=[seq_spec, seq_spec,
                  pl.BlockSpec((1, tc), lambda j, i: (0, j)),
                  pl.BlockSpec((None, kept, 2 * n2, tc), lambda j, i: (order, 0, 0, j),
                               pipeline_mode=pl.Buffered(1)),
                  _resident(tabs["fk"].shape), _resident(tabs["fki"].shape),
                  _resident(gf.shape), _resident(tabs["gi"].shape)],
        out_specs=seq_spec,
        out_shape=jax.ShapeDtypeStruct((b, rows, groups, SUB, c), F32),
        scratch_shapes=[pltpu.VMEM((kept, 2, groups, SUB, tc), F32)],
        compiler_params=_cparams("parallel", "arbitrary"),
        name="hyena_conv",
    )(view(z), view(x), d.reshape(1, c), kf, tabs["fk"], tabs["fki"], gf, tabs["gi"])
    return out.reshape(b, seq_len, c)


def hyena_mixer(proj, batch, seq_len, tabs, w_sc, b_sc, w1, b1, fr1, w2, b2, fr2, w3, hy_d):
    c = HY_WIDTH
    kc = hyena_filter(seq_len, w1, b1, fr1, w2, b2, fr2, w3)
    tc = 2 * LANES if seq_len >= 4096 else c
    kf = filter_spectrum(kc, tabs, tc)
    v, x1, x2 = short_conv(proj, w_sc, b_sc, seq_len)
    z = v.reshape(batch, seq_len, c)
    for o, x in enumerate((x1, x2)):
        z = hyena_conv(z, x.reshape(batch, seq_len, c), hy_d[o], kf, o, tabs, tc)
    return z.reshape(batch * seq_len, c)


def _pool_kernel(u_ref, w_ref, s_ref, o_ref, pad_ref):
    seq_len = u_ref.shape[0]
    h = POOL_HALO
    zeros = jnp.zeros((h, POOL_WIDTH), F32)
    pad_ref[0:h, :] = zeros
    pad_ref[h + seq_len:h + seq_len + h, :] = zeros
    pad_ref[h:h + seq_len, :] = u_ref[...].astype(F32)
    t = lax.broadcasted_iota(jnp.int32, (seq_len, 1), 0)
    for g, w in enumerate(POOL_WINDOWS):
        cs = slice(g * POOL_GW, (g + 1) * POOL_GW)
        tot = pad_ref[h - w // 2:h - w // 2 + seq_len, cs]
        for d in range(1 - w // 2, w - w // 2):
            tot = tot + pad_ref[h + d:h + d + seq_len, cs]
        cnt = (jnp.minimum(t - w // 2 + w, seq_len) - jnp.maximum(t - w // 2, 0)).astype(F32)
        pooled = tot / cnt - pad_ref[h:h + seq_len, cs]
        o_ref[:, cs] = (_dot(pooled.astype(BF16), w_ref[g]) * s_ref[:, cs]).astype(o_ref.dtype)


def pool_mixer(proj, w_pool, pool_scale, batch, seq_len):
    t = proj.shape[0]
    cb = (IN_COLS - POOL_WIDTH) // POOL_WIDTH
    return pl.pallas_call(
        _pool_kernel,
        grid=(batch,),
        in_specs=[pl.BlockSpec((seq_len, POOL_WIDTH), lambda b: (b, cb)),
                  pl.BlockSpec((len(POOL_WINDOWS), POOL_GW, POOL_GW), lambda b: (0, 0, 0)),
                  pl.BlockSpec((1, POOL_WIDTH), lambda b: (0, 0))],
        out_specs=pl.BlockSpec((seq_len, POOL_WIDTH), lambda b: (b, 0)),
        out_shape=jax.ShapeDtypeStruct((t, POOL_WIDTH), BF16),
        scratch_shapes=[pltpu.VMEM((seq_len + 2 * POOL_HALO, POOL_WIDTH), F32)],
        compiler_params=_cparams("parallel"),
        name="pool_mixer",
    )(proj, w_pool.astype(BF16), pool_scale.reshape(1, POOL_WIDTH))


def _outproj_kernel(att_ref, hy_ref, po_ref, x_ref, wa_ref, wh_ref, wp_ref, g1_ref, g_ref, sc_ref, sh_ref,
                    x1_ref, h2_ref):
    mix = (_dot(att_ref[...], wa_ref[...]) + _dot(hy_ref[...].astype(BF16), wh_ref[...])
           + _dot(po_ref[...], wp_ref[...]))
    x1 = x_ref[...] + g1_ref[...] * mix
    x1_ref[...] = x1
    y = x1 * lax.rsqrt(jnp.mean(x1 * x1, axis=-1, keepdims=True) + EPS) * g_ref[...]
    h2_ref[...] = (y * (1.0 + sc_ref[...]) + sh_ref[...]).astype(BF16)


def out_proj(att, hy, po, x, w_out, norm_g, mod, row_of_token, tm=512):
    t, d = x.shape
    a0, a1 = ATT_WIDTH, ATT_WIDTH + HY_WIDTH
    const = lambda shape: pl.BlockSpec(shape, lambda i: (0, 0))
    return pl.pallas_call(
        _outproj_kernel,
        grid=(t // tm,),
        in_specs=[
            pl.BlockSpec((tm, ATT_WIDTH), lambda i: (i, 0)),
            pl.BlockSpec((tm, HY_WIDTH), lambda i: (i, 0)),
            pl.BlockSpec((tm, POOL_WIDTH), lambda i: (i, 0)),
            pl.BlockSpec((tm, d), lambda i: (i, 0)),
            const((ATT_WIDTH, d)),
            pl.BlockSpec((HY_WIDTH, d), lambda i: (a0 // HY_WIDTH, 0)),
            pl.BlockSpec((POOL_WIDTH, d), lambda i: (a1 // POOL_WIDTH, 0)),
            _mod_spec(2, row_of_token, tm),
            const((1, d)),
            _mod_spec(4, row_of_token, tm),
            _mod_spec(3, row_of_token, tm),
        ],
        out_specs=[pl.BlockSpec((tm, d), lambda i: (i, 0))] * 2,
        out_shape=[jax.ShapeDtypeStruct((t, d), F32), jax.ShapeDtypeStruct((t, d), BF16)],
        compiler_params=_cparams("parallel"),
        name="out_proj",
    )(att, hy, po, x, w_out, w_out, w_out, mod, norm_g.reshape(1, d), mod, mod)


def _ffn_kernel(h_ref, x_ref, wg_ref, wu_ref, wd_ref, g2_ref, o_ref, acc_ref):
    j = pl.program_id(1)

    @pl.when(j == 0)
    def _():
        acc_ref[...] = jnp.zeros_like(acc_ref)

    h = h_ref[...]
    gate = _dot(h, wg_ref[...])
    up = _dot(h, wu_ref[...])
    act = (gate * (1.0 / (1.0 + jnp.exp(-gate))) * up).astype(BF16)
    acc_ref[...] += _dot(act, wd_ref[...])

    @pl.when(j == pl.num_programs(1) - 1)
    def _():
        o_ref[...] = x_ref[...] + g2_ref[...] * acc_ref[...]


def ffn(h2, x1, w_gate, w_up, w_down, mod, row_of_token, tm=512, th=512):
    t, d = x1.shape
    hidden = w_gate.shape[1]
    tm = min(tm, t)
    return pl.pallas_call(
        _ffn_kernel,
        grid=(t // tm, hidden // th),
        in_specs=[
            pl.BlockSpec((tm, d), lambda i, j: (i, 0)),
            pl.BlockSpec((tm, d), lambda i, j: (i, 0)),
            pl.BlockSpec((d, th), lambda i, j: (0, j)),
            pl.BlockSpec((d, th), lambda i, j: (0, j)),
            pl.BlockSpec((th, d), lambda i, j: (j, 0)),
            _mod_spec(5, row_of_token, tm),
        ],
        out_specs=pl.BlockSpec((tm, d), lambda i, j: (i, 0)),
        out_shape=jax.ShapeDtypeStruct((t, d), F32),
        scratch_shapes=[pltpu.VMEM((tm, d), F32)],
        compiler_params=_cparams("parallel", "arbitrary"),
        name="ffn",
    )(h2, x1, w_gate, w_up, w_down, mod)


def trunk_layer(x, batch, seq_len, mod, row_of_token, rope_tabs, cache, layer, lam_init, tabs, p, emit_state):
    proj = in_proj(x, p["norm1"], mod, row_of_token, p["w_in"], F32 if emit_state else BF16)
    prep = qk_prep(proj, p["q_norm"], p["k_norm"], rope_tabs, seq_len, emit_state)
    cache_k, cache_v = cache if cache is not None else (None, None)
    att = diff_attn(prep[0], prep[1], proj, cache_k, cache_v, layer, p["w_lambda"], p["subln"], lam_init,
                    batch, seq_len)
    hy = hyena_mixer(proj, batch, seq_len, tabs, p["w_sc"], p["b_sc"], p["hy_w1"], p["hy_b1"], p["hy_fr1"],
                     p["hy_w2"], p["hy_b2"], p["hy_fr2"], p["hy_w3"], p["hy_d"])
    po = pool_mixer(proj, p["w_pool"], p["pool_scale"], batch, seq_len)
    x1, h2 = out_proj(att, hy, po, x, p["w_out"], p["norm2"], mod, row_of_token)
    y = ffn(h2, x1, p["w_gate"], p["w_up"], p["w_down"], mod, row_of_token)
    if emit_state:
        return y, prep[2], proj[:, 2 * ATT_WIDTH:3 * ATT_WIDTH]
    return y, None, None


def kernel(x_prompt, x_sample, cache_k, cache_v, c, c_ctx, w_ada, b_ada, norm1, norm2, w_in, q_norm, k_norm, w_lambda, subln, w_sc, b_sc, hy_w1, hy_b1, hy_fr1, hy_w2, hy_b2, hy_fr2, hy_w3, hy_d, w_pool, pool_scale, w_out, w_gate, w_up, w_down):
    batch, seq, d = x_prompt.shape
    dec_batch, dec_seq, _ = x_sample.shape
    depth = w_ada.shape[0]
    past = cache_k.shape[2]

    cond = jnp.concatenate([c_ctx[None, :], c, jnp.zeros((MOD_ROWS - 1 - dec_batch, d), F32)], axis=0)
    mod_all = ada_mod(cond, w_ada, b_ada)
    rope_tabs = rope_tables(dec_seq)
    tabs_ctx, tabs_lat = dft_tables(seq), dft_tables(dec_seq)
    ck = cache_k.reshape(dec_batch, depth, past, ATT_WIDTH)
    cv = cache_v.reshape(dec_batch, depth, past, ATT_WIDTH)

    y_p = x_prompt.reshape(batch * seq, d)
    y_s = x_sample.reshape(dec_batch * dec_seq, d)
    ks, vs = [], []
    for l in range(depth):
        p = dict(norm1=norm1[l], norm2=norm2[l], w_in=w_in[l].astype(BF16), q_norm=q_norm[l], k_norm=k_norm[l],
                 w_lambda=w_lambda[l], subln=subln[l], w_sc=w_sc[l], b_sc=b_sc[l], hy_w1=hy_w1[l],
                 hy_b1=hy_b1[l], hy_fr1=hy_fr1[l], hy_w2=hy_w2[l], hy_b2=hy_b2[l], hy_fr2=hy_fr2[l],
                 hy_w3=hy_w3[l], hy_d=hy_d[l], w_pool=w_pool[l], pool_scale=pool_scale[l],
                 w_out=w_out[l].astype(BF16), w_gate=w_gate[l].astype(BF16), w_up=w_up[l].astype(BF16),
                 w_down=w_down[l].astype(BF16))
        lam_init = 0.8 - 0.6 * math.exp(-0.3 * l)
        mod = mod_all[l].reshape(MOD_ROWS * 6, 1, d)
        y_p, k_l, v_l = trunk_layer(y_p, batch, seq, mod, lambda tok: 0, None, None, l, lam_init,
                                    tabs_ctx, p, True)
        ks.append(k_l.reshape(batch, seq, N_HEADS, 2, ATT_HD))
        vs.append(v_l.reshape(batch, seq, N_HEADS, ATT_VD))
        y_s, _, _ = trunk_layer(y_s, dec_batch, dec_seq, mod, lambda tok: 1 + tok // dec_seq, rope_tabs,
                                (ck, cv), l, lam_init, tabs_lat, p, False)
    return (y_p.reshape(batch, seq, d), y_s.reshape(dec_batch, dec_seq, d),
            jnp.stack(ks, axis=1), jnp.stack(vs, axis=1))
```
